```python
import math
import jax, jax.numpy as jnp
from jax import lax
import numpy as np

D_MODEL = 1024
BATCH = 4
SEQ = 8192
DEPTH = 4

GRID_W = 64
FN_GROUPS = 4
FN_GROUP_DIM = 64
FN_WIDTH = FN_GROUPS * FN_GROUP_DIM
NA_HEADS = 8
NA_HEAD_DIM = 64
NA_WIDTH = NA_HEADS * NA_HEAD_DIM
NA_KR_MAX = 8
NA_KC = 16
NA_QC = 16
NA_KCB = 2 * NA_KC
NA_NCB = GRID_W // NA_QC
NEG_INF = -1e30
SSM_GROUPS = 16
SSM_GROUP_DIM = 16
SSM_WIDTH = SSM_GROUPS * SSM_GROUP_DIM
SSM_STATE = 64
DT_MIN = 1e-3
DT_MAX = 1e-1
N_BRANCH = 3
D_FF = 4 * D_MODEL
D_IN = FN_WIDTH + 3 * NA_WIDTH + SSM_WIDTH + N_BRANCH * D_MODEL
RMS_EPS = 1e-6

kernel_name = 'hybrid_fnet_natten_s5_encoder'


def rms_norm(x, g):
    xf = x.astype(jnp.float32)
    y = xf * lax.rsqrt(jnp.mean(xf * xf, axis=-1, keepdims=True) + RMS_EPS)
    return (y * g.astype(jnp.float32)).astype(x.dtype)


def fourier_mix(u):
    b, s, _ = u.shape
    ug = u.astype(jnp.float32).reshape(b, s, FN_GROUPS, FN_GROUP_DIM)
    f = jnp.fft.fftn(ug, axes=(1, 3), norm='ortho')
    return jnp.real(f).reshape(b, s, FN_WIDTH).astype(u.dtype)


def _window_starts(n_pos, n_win):
    pos = np.arange(n_pos)
    return np.clip(pos - n_win // 2, 0, n_pos - n_win)


def neighbourhood_attention(q, k, v, rpb):
    b, s, _ = q.shape
    rows = s // GRID_W
    kr = min(NA_KR_MAX, rows)

    def grid(t):
        return t.reshape(b, rows, GRID_W, NA_HEADS, NA_HEAD_DIM).transpose(0, 3, 1, 2, 4)

    qg, kg, vg = grid(q), grid(k), grid(v)
    key_rows = _window_starts(rows, kr)[:, None] + np.arange(kr)
    qcol = np.arange(GRID_W).reshape(NA_NCB, NA_QC)
    blk_start = np.clip(np.arange(NA_NCB) * NA_QC - NA_KC // 2, 0, GRID_W - NA_KCB)
    key_cols = blk_start[:, None] + np.arange(NA_KCB)
    win_start = _window_starts(GRID_W, NA_KC)[qcol]
    in_win = ((key_cols[:, None, :] >= win_start[:, :, None])
              & (key_cols[:, None, :] < win_start[:, :, None] + NA_KC))

    ri = key_rows[:, None, :, None]
    ci = key_cols[None, :, None, :]
    kb = kg[:, :, ri, ci]
    vb = vg[:, :, ri, ci]
    qb = qg.reshape(b, NA_HEADS, rows, NA_NCB, NA_QC, NA_HEAD_DIM)

    scores = jnp.einsum('bhrjqd,bhrjkcd->bhrjqkc', qb, kb).astype(jnp.float32)
    dr = key_rows - np.arange(rows)[:, None] + NA_KR_MAX - 1
    dc = np.clip(key_cols[:, None, :] - qcol[:, :, None], -(NA_KC - 1), NA_KC - 1) + NA_KC - 1
    bias = rpb[:, dr[:, None, None, :, None], dc[None, :, :, None, :]].astype(jnp.float32)
    scores = jnp.where(in_win[:, :, None, :], scores + bias, NEG_INF)
    p = jax.nn.softmax(scores, axis=(-2, -1)).astype(vb.dtype)
    o = jnp.einsum('bhrjqkc,bhrjkcd->bhrjqd', p, vb)
    o = o.reshape(b, NA_HEADS, rows, GRID_W, NA_HEAD_DIM).transpose(0, 2, 3, 1, 4)
    return o.reshape(b, s, NA_WIDTH)


def _ssm_scan(ug, a_re, a_im, log_dt, b_re, b_im, c_re, c_im, reverse):
    lam = lax.complex(a_re.astype(jnp.float32), a_im.astype(jnp.float32))
    dt = jnp.exp(log_dt.astype(jnp.float32))[:, None]
    lam_bar = jnp.exp(lam * dt)
    b_bar = ((lam_bar - 1.0) / lam)[:, :, None] * lax.complex(
        b_re.astype(jnp.float32), b_im.astype(jnp.float32))
    bu = jnp.einsum('bsgc,gpc->bsgp', ug.astype(jnp.complex64), b_bar)
    a = jnp.broadcast_to(lam_bar, bu.shape)

    def combine(left, right):
        a_l, x_l = left
        a_r, x_r = right
        return a_l * a_r, a_r * x_l + x_r

    _, xs = lax.associative_scan(combine, (a, bu), axis=1, reverse=reverse)
    c = lax.complex(c_re.astype(jnp.float32), c_im.astype(jnp.float32))
    return jnp.real(jnp.einsum('bsgp,gcp->bsgc', xs, c))


def ssm_branch(u, a_re, a_im, log_dt, b_re, b_im, c_re, c_im, d_skip, w_glu):
    b, s, _ = u.shape
    uf = u.astype(jnp.float32)
    ug = uf.reshape(b, s, SSM_GROUPS, SSM_GROUP_DIM)
    y = d_skip.astype(jnp.float32) * uf
    for direction in range(2):
        y = y + _ssm_scan(ug, a_re[direction], a_im[direction], log_dt[direction],
                          b_re[direction], b_im[direction], c_re[direction], c_im[direction],
                          reverse=(direction == 1)).reshape(b, s, SSM_WIDTH)
    y = jax.nn.gelu(y)
    y = y * jax.nn.sigmoid(y @ w_glu.astype(jnp.float32))
    return y.astype(u.dtype)


def setup_inputs(seed: int = 0) -> dict:
    key = jax.random.key(seed)
    ks = jax.random.split(key, 24)
    L = DEPTH

    def nrm(i, shape, scale):
        return scale * jax.random.normal(ks[i], shape, jnp.float32)

    n_idx = jnp.arange(SSM_STATE, dtype=jnp.float32)
    shp_a = (L, 2, SSM_GROUPS, SSM_STATE)
    return {
        'x': nrm(0, (BATCH, SEQ, D_MODEL), 1.0),
        'g_mix': 1.0 + nrm(1, (L, D_MODEL), 0.01),
        'w_in': nrm(2, (L, D_MODEL, D_IN), D_MODEL ** -0.5),
        'na_rpb': nrm(3, (L, NA_HEADS, 2 * NA_KR_MAX - 1, 2 * NA_KC - 1), 0.02),
        'ssm_a_re': -0.5 + nrm(4, shp_a, 0.01),
        'ssm_a_im': math.pi * n_idx + nrm(5, shp_a, 0.01),
        'ssm_log_dt': jax.random.uniform(ks[6], (L, 2, SSM_GROUPS), jnp.float32,
                                         math.log(DT_MIN), math.log(DT_MAX)),
        'ssm_b_re': nrm(7, (L, 2, SSM_GROUPS, SSM_STATE, SSM_GROUP_DIM), (2 * SSM_GROUP_DIM) ** -0.5),
        'ssm_b_im': nrm(8, (L, 2, SSM_GROUPS, SSM_STATE, SSM_GROUP_DIM), (2 * SSM_GROUP_DIM) ** -0.5),
        'ssm_c_re': nrm(9, (L, 2, SSM_GROUPS, SSM_GROUP_DIM, SSM_STATE), 2 ** -0.5),
        'ssm_c_im': nrm(10, (L, 2, SSM_GROUPS, SSM_GROUP_DIM, SSM_STATE), 2 ** -0.5),
        'ssm_d': nrm(11, (L, SSM_WIDTH), 1.0),
        'w_glu': nrm(12, (L, SSM_WIDTH, SSM_WIDTH), SSM_WIDTH ** -0.5),
        'w_br_fn': nrm(13, (L, FN_WIDTH, D_MODEL), FN_WIDTH ** -0.5),
        'w_br_na': nrm(14, (L, NA_WIDTH, D_MODEL), NA_WIDTH ** -0.5),
        'w_br_ssm': nrm(15, (L, SSM_WIDTH, D_MODEL), SSM_WIDTH ** -0.5),
        'w_out': nrm(16, (L, D_MODEL, D_MODEL), D_MODEL ** -0.5),
        'g_ffn': 1.0 + nrm(17, (L, D_MODEL), 0.01),
        'w_up': nrm(18, (L, D_MODEL, D_FF), D_MODEL ** -0.5),
        'w_down': nrm(19, (L, D_FF, D_MODEL), D_FF ** -0.5),
        'g_final': 1.0 + nrm(20, (D_MODEL,), 0.01),
    }


def reference(x, g_mix, w_in, na_rpb, ssm_a_re, ssm_a_im, ssm_log_dt, ssm_b_re, ssm_b_im,
              ssm_c_re, ssm_c_im, ssm_d, w_glu, w_br_fn, w_br_na, w_br_ssm, w_out,
              g_ffn, w_up, w_down, g_final):
    b, s, _ = x.shape
    splits = [FN_WIDTH, FN_WIDTH + NA_WIDTH, FN_WIDTH + 2 * NA_WIDTH,
              FN_WIDTH + 3 * NA_WIDTH, FN_WIDTH + 3 * NA_WIDTH + SSM_WIDTH]
    q_scale = NA_HEAD_DIM ** -0.5
    for l in range(DEPTH):
        h = rms_norm(x, g_mix[l])
        z = h @ w_in[l]
        u_fn, q, k, v, u_ssm, gate_logits = jnp.split(z, splits, axis=-1)
        gates = jax.nn.sigmoid(gate_logits.reshape(b, s, N_BRANCH, D_MODEL))
        y_fn = fourier_mix(u_fn) @ w_br_fn[l]
        y_na = neighbourhood_attention(q * q_scale, k, v, na_rpb[l]) @ w_br_na[l]
        y_ssm = ssm_branch(u_ssm, ssm_a_re[l], ssm_a_im[l], ssm_log_dt[l], ssm_b_re[l], ssm_b_im[l],
                           ssm_c_re[l], ssm_c_im[l], ssm_d[l], w_glu[l]) @ w_br_ssm[l]
        merged = gates[:, :, 0] * y_fn + gates[:, :, 1] * y_na + gates[:, :, 2] * y_ssm
        x = x + merged @ w_out[l]
        h = rms_norm(x, g_ffn[l])
        x = x + jnp.square(jax.nn.relu(h @ w_up[l])) @ w_down[l]
    return rms_norm(x, g_final)
```

```python
import functools
import math

import numpy as np
import jax
import jax.numpy as jnp
from jax import lax
from jax.experimental import pallas as pl
from jax.experimental.pallas import tpu as pltpu

F32 = jnp.float32
BF16 = jnp.bfloat16

D_MODEL = 1024
GRID_W = 64
FN_GROUP_DIM = 64
FN_WIDTH = 256
NA_HEADS = 8
NA_HEAD_DIM = 64
NA_WIDTH = 512
NA_KR = 8
NA_KC = 16
NEG_INF = -1e30
SSM_GROUPS = 16
SSM_GROUP_DIM = 16
SSM_WIDTH = 256
SSM_STATE = 64
N_BRANCH = 3
D_FF = 4 * D_MODEL
D_IN = FN_WIDTH + 3 * NA_WIDTH + SSM_WIDTH + N_BRANCH * D_MODEL
RMS_EPS = 1e-6

DFT_N1 = 128
DFT_N2 = 64
NA_RQ = 4
NA_QB = NA_RQ * GRID_W
SSM_CHUNK = 16
SSM_PAIRS = SSM_GROUPS // 2
SSM_CK = SSM_CHUNK * SSM_GROUP_DIM

TOKEN_TILE = 512
VMEM_LIMIT = 56 * 1024 * 1024


def _cparams(n_axes):
    return pltpu.CompilerParams(dimension_semantics=("arbitrary",) * n_axes, vmem_limit_bytes=VMEM_LIMIT)


def _resident(block_shape, index_map):
    return pl.BlockSpec(block_shape, index_map, pipeline_mode=pl.Buffered(1))


def _rms(x, g):
    ms = jnp.mean(x * x, axis=-1, keepdims=True)
    return x * lax.rsqrt(ms + RMS_EPS) * g


def _inproj_kernel(x_ref, g_ref, w_ref, ufn_ref, q_ref, k_ref, v_ref, ussm_ref, gate_ref):
    h = _rms(x_ref[...], g_ref[...]).astype(BF16)

    def seg(lo, width):
        return jnp.dot(h, w_ref[:, lo:lo + width], preferred_element_type=F32)

    ufn_ref[...] = seg(0, FN_WIDTH).astype(BF16)
    q_ref[...] = (seg(FN_WIDTH, NA_WIDTH) * (NA_HEAD_DIM ** -0.5)).astype(BF16)
    k_ref[...] = seg(FN_WIDTH + NA_WIDTH, NA_WIDTH).astype(BF16)
    v_ref[...] = seg(FN_WIDTH + 2 * NA_WIDTH, NA_WIDTH).astype(BF16)
    ussm_ref[...] = seg(FN_WIDTH + 3 * NA_WIDTH, SSM_WIDTH)
    gate_lo = FN_WIDTH + 3 * NA_WIDTH + SSM_WIDTH
    for j in range(N_BRANCH * D_MODEL // 512):
        gate_ref[:, j * 512:(j + 1) * 512] = jax.nn.sigmoid(seg(gate_lo + j * 512, 512)).astype(BF16)


def _inproj(x, g_mix, w_in, layer):
    n = x.shape[0]
    tm = TOKEN_TILE
    row = lambda width: pl.BlockSpec((tm, width), lambda i: (i, 0))
    return pl.pallas_call(
        _inproj_kernel,
        grid=(n // tm,),
        in_specs=[row(D_MODEL),
                  _resident((None, 1, D_MODEL), lambda i: (layer, 0, 0)),
                  _resident((None, D_MODEL, D_IN), lambda i: (layer, 0, 0))],
        out_specs=[row(FN_WIDTH), row(NA_WIDTH), row(NA_WIDTH), row(NA_WIDTH), row(SSM_WIDTH),
                   row(N_BRANCH * D_MODEL)],
        out_shape=[jax.ShapeDtypeStruct((n, FN_WIDTH), BF16),
                   jax.ShapeDtypeStruct((n, NA_WIDTH), BF16),
                   jax.ShapeDtypeStruct((n, NA_WIDTH), BF16),
                   jax.ShapeDtypeStruct((n, NA_WIDTH), BF16),
                   jax.ShapeDtypeStruct((n, SSM_WIDTH), F32),
                   jax.ShapeDtypeStruct((n, N_BRANCH * D_MODEL), BF16)],
        compiler_params=_cparams(1),
        name="inproj",
    )(x, g_mix, w_in)


def _dft_tables(seq):
    two_pi = 2.0 * math.pi
    k1 = lax.broadcasted_iota(jnp.int32, (DFT_N1, DFT_N1), 0)
    n1 = lax.broadcasted_iota(jnp.int32, (DFT_N1, DFT_N1), 1)
    ang = ((k1 * n1) % DFT_N1).astype(F32) * (two_pi / DFT_N1)
    wa = jnp.concatenate([jnp.cos(ang), -jnp.sin(ang)], axis=0)

    shape = (DFT_N1, DFT_N2, DFT_N2)
    kk1 = lax.broadcasted_iota(jnp.int32, shape, 0)
    kk2 = lax.broadcasted_iota(jnp.int32, shape, 1)
    nn2 = lax.broadcasted_iota(jnp.int32, shape, 2)
    ang = ((nn2 * (kk1 + DFT_N1 * kk2)) % seq).astype(F32) * (two_pi / seq)
    c, s = jnp.cos(ang), jnp.sin(ang)
    g = jnp.concatenate([jnp.concatenate([c, s], axis=2), jnp.concatenate([-s, c], axis=2)], axis=1)

    m = lax.broadcasted_iota(jnp.int32, (FN_WIDTH, FN_WIDTH), 0)
    j = lax.broadcasted_iota(jnp.int32, (FN_WIDTH, FN_WIDTH), 1)
    same_group = (m // FN_GROUP_DIM) == (j // FN_GROUP_DIM)
    ang = ((m * j) % FN_GROUP_DIM).astype(F32) * (two_pi / FN_GROUP_DIM)
    scale = 1.0 / math.sqrt(seq * FN_GROUP_DIM)
    cc = jnp.where(same_group, jnp.cos(ang) * scale, 0.0)
    sc = jnp.where(same_group, jnp.sin(ang) * scale, 0.0)
    return wa.astype(BF16), g.astype(BF16), cc.astype(BF16), sc.astype(BF16)


def _dft_a_kernel(w_ref, x_ref, o_ref):
    o_ref[...] = jnp.dot(w_ref[...], x_ref[...], preferred_element_type=F32).astype(BF16)


DFT_A_TILE = 2048
DFT_C_K1 = 8


def _dft_c_kernel(ar_ref, ai_ref, g_ref, cc_ref, sc_ref, o_ref):
    for j in range(DFT_C_K1):
        a = jnp.concatenate([ar_ref[j], ai_ref[j]], axis=0)
        z = jnp.dot(g_ref[j], a, preferred_element_type=F32)
        zr = z[:DFT_N2].astype(BF16)
        zi = z[DFT_N2:].astype(BF16)
        o = (jnp.dot(zr, cc_ref[...], preferred_element_type=F32)
             + jnp.dot(zi, sc_ref[...], preferred_element_type=F32))
        o_ref[j] = o.astype(BF16)


def _fourier_mix(ufn, batch, seq, tables):
    wa, g, cc, sc = tables
    cols = DFT_N2 * FN_WIDTH
    x = ufn.reshape(batch, DFT_N1, cols)
    a = pl.pallas_call(
        _dft_a_kernel,
        grid=(batch, cols // DFT_A_TILE),
        in_specs=[_resident((2 * DFT_N1, DFT_N1), lambda b, j: (0, 0)),
                  pl.BlockSpec((None, DFT_N1, DFT_A_TILE), lambda b, j: (b, 0, j))],
        out_specs=pl.BlockSpec((None, 2 * DFT_N1, DFT_A_TILE), lambda b, j: (b, 0, j)),
        out_shape=jax.ShapeDtypeStruct((batch, 2 * DFT_N1, cols), BF16),
        compiler_params=_cparams(2),
        name="dft_a",
    )(wa, x)
    a = a.reshape(batch, 2, DFT_N1, DFT_N2, FN_WIDTH)
    kb = DFT_C_K1
    part = lambda p: pl.BlockSpec((None, None, kb, DFT_N2, FN_WIDTH), lambda b, i: (b, p, i, 0, 0))
    out = pl.pallas_call(
        _dft_c_kernel,
        grid=(batch, DFT_N1 // kb),
        in_specs=[part(0), part(1),
                  pl.BlockSpec((kb, 2 * DFT_N2, 2 * DFT_N2), lambda b, i: (i, 0, 0)),
                  _resident((FN_WIDTH, FN_WIDTH), lambda b, i: (0, 0)),
                  _resident((FN_WIDTH, FN_WIDTH), lambda b, i: (0, 0))],
        out_specs=pl.BlockSpec((None, kb, DFT_N2, FN_WIDTH), lambda b, i: (b, i, 0, 0)),
        out_shape=jax.ShapeDtypeStruct((batch, DFT_N1, DFT_N2, FN_WIDTH), BF16),
        compiler_params=_cparams(2),
        name="dft_c",
    )(a, a, g, cc, sc)
    return out.transpose(0, 2, 1, 3).reshape(batch * seq, FN_WIDTH)


def _na_index_tables(rows):
    kr = min(NA_KR, rows)
    n_blocks = rows // NA_RQ
    qa, qc = np.divmod(np.arange(NA_QB), GRID_W)
    kj, kc = np.divmod(np.arange(3 * NA_QB), GRID_W)
    col_start = np.clip(qc - NA_KC // 2, 0, GRID_W - NA_KC)
    col_ok = (kc[None, :] >= col_start[:, None]) & (kc[None, :] < col_start[:, None] + NA_KC)
    dc = np.clip(kc[None, :] - qc[:, None], -(NA_KC - 1), NA_KC - 1) + NA_KC - 1
    dr_all, ok_all = [], []
    for blk in (0, 1, n_blocks - 1):
        r = blk * NA_RQ + qa
        key_row = (blk - 1) * NA_RQ + kj
        row_start = np.clip(r - kr // 2, 0, rows - kr)
        row_ok = (key_row[None, :] >= row_start[:, None]) & (key_row[None, :] < row_start[:, None] + kr)
        dr = np.clip(key_row[None, :] - r[:, None] + NA_KR - 1, 0, 2 * NA_KR - 2)
        dr_all.append(dr)
        ok_all.append(row_ok & col_ok)
    dc_all = np.broadcast_to(dc, (3,) + dc.shape)
    return np.stack(dr_all), dc_all, np.stack(ok_all)


def _na_bias_table(rpb, rows):
    dr, dc, ok = _na_index_tables(rows)
    bias = rpb[:, dr, dc]
    return jnp.where(ok[None], bias, NEG_INF).transpose(1, 0, 2, 3).astype(F32)


def _na_kernel(q_ref, kp_ref, kc_ref, kn_ref, vp_ref, vc_ref, vn_ref, bm_ref, o_ref):
    lane = lax.broadcasted_iota(jnp.int32, (1, 2 * NA_HEAD_DIM), 1)
    k_refs = (kp_ref, kc_ref, kn_ref)
    v_refs = (vp_ref, vc_ref, vn_ref)
    for hp in range(NA_HEADS // 2):
        sl = slice(2 * NA_HEAD_DIM * hp, 2 * NA_HEAD_DIM * (hp + 1))
        q2 = q_ref[:, sl]
        ks = [r[:, sl] for r in k_refs]
        vs = [r[:, sl] for r in v_refs]
        outs = []
        for e in range(2):
            in_head = (lane >= NA_HEAD_DIM * e) & (lane < NA_HEAD_DIM * (e + 1))
            qm = jnp.where(in_head, q2, jnp.zeros_like(q2))
            s = [lax.dot_general(qm, kk, (((1,), (1,)), ((), ())), preferred_element_type=F32)
                 + bm_ref[2 * hp + e, :, NA_QB * i:NA_QB * (i + 1)]
                 for i, kk in enumerate(ks)]
            m = jnp.maximum(jnp.maximum(jnp.max(s[0], axis=-1, keepdims=True),
                                        jnp.max(s[1], axis=-1, keepdims=True)),
                            jnp.max(s[2], axis=-1, keepdims=True))
            p = [jnp.exp(si - m) for si in s]
            denom = (jnp.sum(p[0], axis=-1, keepdims=True) + jnp.sum(p[1], axis=-1, keepdims=True)
                     + jnp.sum(p[2], axis=-1, keepdims=True))
            pv = (jnp.dot(p[0].astype(BF16), vs[0], preferred_element_type=F32)
                  + jnp.dot(p[1].astype(BF16), vs[1], preferred_element_type=F32)
                  + jnp.dot(p[2].astype(BF16), vs[2], preferred_element_type=F32))
            outs.append(pv / denom)
        o_ref[:, sl] = jnp.where(lane < NA_HEAD_DIM, outs[0], outs[1]).astype(BF16)


def _neighbourhood_attention(q, k, v, bias_table, batch, seq):
    n_blocks = seq // NA_QB
    q3, k3, v3 = (t.reshape(batch, seq, NA_WIDTH) for t in (q, k, v))
    blk = lambda shift: pl.BlockSpec(
        (None, NA_QB, NA_WIDTH), lambda b, i: (b, jnp.clip(i + shift, 0, n_blocks - 1), 0))
    variant = lambda b, i: (jnp.where(i == 0, 0, jnp.where(i == n_blocks - 1, 2, 1)), 0, 0, 0)
    out = pl.pallas_call(
        _na_kernel,
        grid=(batch, n_blocks),
        in_specs=[blk(0), blk(-1), blk(0), blk(1), blk(-1), blk(0), blk(1),
                  pl.BlockSpec((None, NA_HEADS, NA_QB, 3 * NA_QB), variant)],
        out_specs=blk(0),
        out_shape=jax.ShapeDtypeStruct((batch, seq, NA_WIDTH), BF16),
        compiler_params=_cparams(2),
        name="natten",
    )(q3, k3, k3, k3, v3, v3, v3, bias_table)
    return out.reshape(batch * seq, NA_WIDTH)


def _cmul(ar, ai, br, bi):
    return ar * br - ai * bi, ar * bi + ai * br


def _ssm_tables(a_re, a_im, log_dt, b_re, b_im, c_re, c_im):
    hi = lax.Precision.HIGHEST
    dt = jnp.exp(log_dt)[..., None]
    zr, zi = a_re * dt, a_im * dt
    taus = jnp.arange(SSM_CHUNK + 1, dtype=F32)[:, None, None, None]
    mag = jnp.exp(zr[None] * taus)
    pr, pi = mag * jnp.cos(zi[None] * taus), mag * jnp.sin(zi[None] * taus)
    lbr, lbi = pr[1], pi[1]
    den = a_re * a_re + a_im * a_im
    fr = ((lbr - 1.0) * a_re + lbi * a_im) / den
    fi = (lbi * a_re - (lbr - 1.0) * a_im) / den
    bbr, bbi = _cmul(fr[..., None], fi[..., None], b_re, b_im)

    wr, wi = _cmul(c_re[:, None], c_im[:, None],
                   pr[:SSM_CHUNK].transpose(1, 0, 2, 3)[:, :, :, None, :],
                   pi[:SSM_CHUNK].transpose(1, 0, 2, 3)[:, :, :, None, :])
    kern = (jnp.einsum('dtgop,dgpc->dtgoc', wr, bbr, precision=hi)
            - jnp.einsum('dtgop,dgpc->dtgoc', wi, bbi, precision=hi))
    kf, kb = kern[0], kern[1]
    kmix = jnp.concatenate([kb[:0:-1], (kf[0] + kb[0])[None], kf[1:]], axis=0)
    t_idx = np.arange(SSM_CHUNK)
    delta = t_idx[None, :] - t_idx[:, None] + SSM_CHUNK - 1
    intra = kmix[delta]
    intra = intra.transpose(2, 0, 4, 1, 3).reshape(SSM_GROUPS, SSM_CK, SSM_CK)

    pf_r, pf_i = _cmul(pr[SSM_CHUNK - 1::-1, 0][..., None], pi[SSM_CHUNK - 1::-1, 0][..., None],
                       bbr[0][None], bbi[0][None])
    pb_r, pb_i = _cmul(pr[:SSM_CHUNK, 1][..., None], pi[:SSM_CHUNK, 1][..., None], bbr[1][None], bbi[1][None])
    sp = jnp.stack([pf_r, pf_i, pb_r, pb_i], axis=0)
    sp = sp.transpose(2, 1, 4, 0, 3).reshape(SSM_GROUPS, SSM_CK, 4, SSM_STATE)
    sp = sp.reshape(SSM_PAIRS, 2, SSM_CK, 4, 1, SSM_STATE)
    eye = jnp.eye(2, dtype=F32).reshape(1, 2, 1, 1, 2, 1)
    sproj = (sp * eye).reshape(SSM_PAIRS, 2, SSM_CK, 4 * 2 * SSM_STATE)

    qf_r, qf_i = _cmul(c_re[0][None], c_im[0][None],
                       pr[1:, 0][:, :, None, :], pi[1:, 0][:, :, None, :])
    qb_r, qb_i = _cmul(c_re[1][None], c_im[1][None],
                       pr[:0:-1, 1][:, :, None, :], pi[:0:-1, 1][:, :, None, :])
    qp = jnp.stack([qf_r, -qf_i, qb_r, -qb_i], axis=0)
    qp = qp.transpose(2, 0, 4, 1, 3).reshape(SSM_PAIRS, 2, 4, SSM_STATE, 1, SSM_CK)
    eye = jnp.eye(2, dtype=F32).reshape(1, 2, 1, 1, 2, 1)
    qproj = (qp * eye).transpose(0, 2, 1, 3, 4, 5).reshape(SSM_PAIRS, 4, 2 * SSM_STATE, 2 * SSM_CK)

    lam_l = jnp.stack([pr[SSM_CHUNK, 0], pi[SSM_CHUNK, 0], pr[SSM_CHUNK, 1], pi[SSM_CHUNK, 1]], axis=0)
    lam_l = jnp.concatenate([lam_l.reshape(4, SSM_GROUPS * SSM_STATE),
                             jnp.zeros((4, SSM_GROUPS * SSM_STATE), F32)], axis=0)
    return intra.astype(BF16), sproj.astype(BF16), qproj.astype(BF16), lam_l


SSM_ROW_TILE = 1024
SSM_SCAN_LANES = 256


def _ssm_state_kernel(u_ref, sp_ref, st_ref):
    st = (jnp.dot(u_ref[0], sp_ref[0], preferred_element_type=F32)
          + jnp.dot(u_ref[1], sp_ref[1], preferred_element_type=F32))
    for part in range(4):
        st_ref[part] = st[:, 128 * part:128 * (part + 1)]


def _ssm_scan_kernel(st_ref, lam_ref, xin_ref):
    n_tiles = st_ref.shape[1] // 8
    lanes = st_ref.shape[2]
    lo = lax.broadcasted_iota(jnp.int32, (8, lanes), 0) < 4
    lfr, lfi, lbr, lbi = (jnp.broadcast_to(lam_ref[p:p + 1, :], (8, lanes)) for p in range(4))

    def swap(x):
        return pltpu.roll(x, 4, 0)

    def body(i, carry):
        xfr, xfi, xbr, xbi = carry
        rf = pl.multiple_of(i * 8, 8)
        sr, si = st_ref[0, pl.ds(rf, 8), :], st_ref[1, pl.ds(rf, 8), :]
        ar, ai = _cmul(lfr, lfi, xfr, xfi)
        ar, ai = ar + sr, ai + si
        er, ei = jnp.where(lo, xfr, swap(ar)), jnp.where(lo, xfi, swap(ai))
        xin_ref[0, pl.ds(rf, 8), :] = er
        xin_ref[1, pl.ds(rf, 8), :] = ei
        br, bi = _cmul(lfr, lfi, er, ei)
        br, bi = br + sr, bi + si
        nfr, nfi = jnp.where(lo, swap(br), br), jnp.where(lo, swap(bi), bi)
        rb = pl.multiple_of((n_tiles - 1 - i) * 8, 8)
        sr, si = st_ref[2, pl.ds(rb, 8), :], st_ref[3, pl.ds(rb, 8), :]
        ar, ai = _cmul(lbr, lbi, xbr, xbi)
        ar, ai = ar + sr, ai + si
        er, ei = jnp.where(lo, swap(ar), xbr), jnp.where(lo, swap(ai), xbi)
        xin_ref[2, pl.ds(rb, 8), :] = er
        xin_ref[3, pl.ds(rb, 8), :] = ei
        br, bi = _cmul(lbr, lbi, er, ei)
        br, bi = br + sr, bi + si
        nbr, nbi = jnp.where(lo, br, swap(br)), jnp.where(lo, bi, swap(bi))
        return nfr, nfi, nbr, nbi

    zero = jnp.zeros((8, lanes), F32)
    lax.fori_loop(0, n_tiles, body, (zero, zero, zero, zero))


def _ssm_out_kernel(u_ref, xin_ref, mi_ref, qp_ref, o_ref):
    acc = jnp.dot(xin_ref[0].astype(BF16), qp_ref[0], preferred_element_type=F32)
    for part in range(1, 4):
        acc = acc + jnp.dot(xin_ref[part].astype(BF16), qp_ref[part], preferred_element_type=F32)
    for e in range(2):
        y0 = jnp.dot(u_ref[e], mi_ref[e], preferred_element_type=F32)
        o_ref[e] = y0 + acc[:, SSM_CK * e:SSM_CK * (e + 1)]


def _ssm_scan(ussm, batch, seq, tables):
    intra, sproj, qproj, lam_l = tables
    n_chunks = seq // SSM_CHUNK
    rows = n_chunks * batch
    n_state = SSM_GROUPS * SSM_STATE
    u = ussm.astype(BF16).reshape(batch, n_chunks, SSM_CHUNK, SSM_GROUPS, SSM_GROUP_DIM)
    u = u.transpose(3, 1, 0, 2, 4).reshape(SSM_GROUPS, rows, SSM_CK)
    tr = SSM_ROW_TILE
    st = pl.pallas_call(
        _ssm_state_kernel,
        grid=(SSM_PAIRS, rows // tr),
        in_specs=[pl.BlockSpec((2, tr, SSM_CK), lambda p, r: (p, r, 0)),
                  pl.BlockSpec((None, 2, SSM_CK, 512), lambda p, r: (p, 0, 0, 0))],
        out_specs=pl.BlockSpec((4, tr, 128), lambda p, r: (0, r, p)),
        out_shape=jax.ShapeDtypeStruct((4, rows, n_state), F32),
        compiler_params=_cparams(2),
        name="ssm_state",
    )(u, sproj)
    lanes = SSM_SCAN_LANES
    xin = pl.pallas_call(
        _ssm_scan_kernel,
        grid=(n_state // lanes,),
        in_specs=[pl.BlockSpec((4, rows, lanes), lambda j: (0, 0, j)),
                  pl.BlockSpec((8, lanes), lambda j: (0, j))],
        out_specs=pl.BlockSpec((4, rows, lanes), lambda j: (0, 0, j)),
        out_shape=jax.ShapeDtypeStruct((4, rows, n_state), F32),
        compiler_params=_cparams(1),
        name="ssm_scan",
    )(st, lam_l)
    y = pl.pallas_call(
        _ssm_out_kernel,
        grid=(SSM_PAIRS, rows // tr),
        in_specs=[pl.BlockSpec((2, tr, SSM_CK), lambda p, r: (p, r, 0)),
                  pl.BlockSpec((4, tr, 128), lambda p, r: (0, r, p)),
                  pl.BlockSpec((2, SSM_CK, SSM_CK), lambda p, r: (p, 0, 0)),
                  pl.BlockSpec((None, 4, 128, 2 * SSM_CK), lambda p, r: (p, 0, 0, 0))],
        out_specs=pl.BlockSpec((2, tr, SSM_CK), lambda p, r: (p, r, 0)),
        out_shape=jax.ShapeDtypeStruct((SSM_GROUPS, rows, SSM_CK), F32),
        compiler_params=_cparams(2),
        name="ssm_out",
    )(u, xin, intra, qproj)
    y = y.reshape(SSM_GROUPS, n_chunks, batch, SSM_CHUNK, SSM_GROUP_DIM)
    return y.transpose(2, 1, 3, 0, 4).reshape(batch * seq, SSM_WIDTH)


def _merge_kernel(x_ref, gate_ref, fm_ref, na_ref, yscan_ref, ussm_ref, dskip_ref, wglu_ref,
                  wfn_ref, wna_ref, wssm_ref, wout_ref, o_ref):
    y = yscan_ref[...] + dskip_ref[...] * ussm_ref[...]
    y = jax.nn.gelu(y)
    y = y * jax.nn.sigmoid(jnp.dot(y.astype(BF16), wglu_ref[...], preferred_element_type=F32))
    y_ssm = jnp.dot(y.astype(BF16), wssm_ref[...], preferred_element_type=F32)
    y_fn = jnp.dot(fm_ref[...], wfn_ref[...], preferred_element_type=F32)
    y_na = jnp.dot(na_ref[...], wna_ref[...], preferred_element_type=F32)
    merged = (gate_ref[:, :D_MODEL].astype(F32) * y_fn
              + gate_ref[:, D_MODEL:2 * D_MODEL].astype(F32) * y_na
              + gate_ref[:, 2 * D_MODEL:].astype(F32) * y_ssm)
    o_ref[...] = x_ref[...] + jnp.dot(merged.astype(BF16), wout_ref[...], preferred_element_type=F32)


def _merge(x, gates, fm, na, yscan, ussm, ssm_d, w_glu, w_br_fn, w_br_na, w_br_ssm, w_out, layer):
    n = x.shape[0]
    tm = TOKEN_TILE
    row = lambda width: pl.BlockSpec((tm, width), lambda i: (i, 0))
    wspec = lambda k, m: _resident((None, k, m), lambda i: (layer, 0, 0))
    return pl.pallas_call(
        _merge_kernel,
        grid=(n // tm,),
        in_specs=[row(D_MODEL), row(N_BRANCH * D_MODEL), row(FN_WIDTH), row(NA_WIDTH), row(SSM_WIDTH),
                  row(SSM_WIDTH), wspec(1, SSM_WIDTH), wspec(SSM_WIDTH, SSM_WIDTH),
                  wspec(FN_WIDTH, D_MODEL), wspec(NA_WIDTH, D_MODEL), wspec(SSM_WIDTH, D_MODEL),
                  wspec(D_MODEL, D_MODEL)],
        out_specs=row(D_MODEL),
        out_shape=jax.ShapeDtypeStruct((n, D_MODEL), F32),
        compiler_params=_cparams(1),
        name="merge",
    )(x, gates, fm, na, yscan, ussm, ssm_d, w_glu, w_br_fn, w_br_na, w_br_ssm, w_out)


def _ffn_kernel(x_ref, g_ref, wup_ref, wdown_ref, gfin_ref, o_ref, up_ref, *, final_norm):
    x = x_ref[...]
    h = _rms(x, g_ref[...]).astype(BF16)
    for c in range(D_FF // D_MODEL):
        cols = slice(c * D_MODEL, (c + 1) * D_MODEL)
        a = jnp.maximum(jnp.dot(h, wup_ref[:, cols], preferred_element_type=F32), 0.0)
        up_ref[:, cols] = (a * a).astype(BF16)
    y = x + jnp.dot(up_ref[...], wdown_ref[...], preferred_element_type=F32)
    if final_norm:
        y = _rms(y, gfin_ref[...])
    o_ref[...] = y


def _ffn(x, g_ffn, w_up, w_down, g_final, layer, final_norm):
    n = x.shape[0]
    tm = TOKEN_TILE
    row = pl.BlockSpec((tm, D_MODEL), lambda i: (i, 0))
    return pl.pallas_call(
        functools.partial(_ffn_kernel, final_norm=final_norm),
        grid=(n // tm,),
        in_specs=[row,
                  _resident((None, 1, D_MODEL), lambda i: (layer, 0, 0)),
                  _resident((None, D_MODEL, D_FF), lambda i: (layer, 0, 0)),
                  _resident((None, D_FF, D_MODEL), lambda i: (layer, 0, 0)),
                  _resident((1, D_MODEL), lambda i: (0, 0))],
        out_specs=row,
        out_shape=jax.ShapeDtypeStruct((n, D_MODEL), F32),
        scratch_shapes=[pltpu.VMEM((tm, D_FF), BF16)],
        compiler_params=_cparams(1),
        name="ffn",
    )(x, g_ffn, w_up, w_down, g_final)


def kernel(x, g_mix, w_in, na_rpb, ssm_a_re, ssm_a_im, ssm_log_dt, ssm_b_re, ssm_b_im, ssm_c_re, ssm_c_im,
           ssm_d, w_glu, w_br_fn, w_br_na, w_br_ssm, w_out, g_ffn, w_up, w_down, g_final):
    batch, seq, _ = x.shape
    depth = w_in.shape[0]
    assert seq == DFT_N1 * DFT_N2 and seq % (GRID_W * NA_RQ) == 0 and (batch * seq) % TOKEN_TILE == 0
    rows = seq // GRID_W

    to_bf16 = lambda w: w.astype(BF16)
    w_in, w_glu, w_br_fn, w_br_na, w_br_ssm, w_out, w_up, w_down = map(
        to_bf16, (w_in, w_glu, w_br_fn, w_br_na, w_br_ssm, w_out, w_up, w_down))
    g_mix3 = g_mix.reshape(depth, 1, D_MODEL)
    g_ffn3 = g_ffn.reshape(depth, 1, D_MODEL)
    ssm_d3 = ssm_d.reshape(depth, 1, SSM_WIDTH)
    g_final2 = g_final.reshape(1, D_MODEL)
    dft_tables = _dft_tables(seq)

    xs = x.reshape(batch * seq, D_MODEL)
    for l in range(depth):
        ufn, q, k, v, ussm, gates = _inproj(xs, g_mix3, w_in, l)
        fm = _fourier_mix(ufn, batch, seq, dft_tables)
        na = _neighbourhood_attention(q, k, v, _na_bias_table(na_rpb[l], rows), batch, seq)
        yscan = _ssm_scan(ussm, batch, seq,
                          _ssm_tables(ssm_a_re[l], ssm_a_im[l], ssm_log_dt[l], ssm_b_re[l], ssm_b_im[l],
                                      ssm_c_re[l], ssm_c_im[l]))
        xs = _merge(xs, gates, fm, na, yscan, ussm, ssm_d3, w_glu, w_br_fn, w_br_na, w_br_ssm, w_out, l)
        xs = _ffn(xs, g_ffn3, w_up, w_down, g_final2, l, final_norm=(l == depth - 1))
    return xs.reshape(batch, seq, D_MODEL)
```

```python
import functools
import math

import numpy as np
import jax
import jax.numpy as jnp
from jax import lax
from jax.experimental import pallas as pl
from jax.experimental.pallas import tpu as pltpu

F32 = jnp.float32
BF16 = jnp.bfloat16

D_MODEL = 1024
GRID_W = 64
FN_GROUP_DIM = 64
FN_WIDTH = 256
NA_HEADS = 8
NA_HEAD_DIM = 64
NA_WIDTH = 512
NA_KR = 8
NA_KC = 16
NEG_INF = -1e30
SSM_GROUPS = 16
SSM_GROUP_DIM = 16
SSM_WIDTH = 256
SSM_STATE = 64
N_BRANCH = 3
D_FF = 4 * D_MODEL
D_IN = FN_WIDTH + 3 * NA_WIDTH + SSM_WIDTH + N_BRANCH * D_MODEL
RMS_EPS = 1e-6

DFT_N1 = 128
DFT_N2 = 64
NA_RQ = 4
NA_QB = NA_RQ * GRID_W
SSM_CHUNK = 16
SSM_PAIRS = SSM_GROUPS // 2
SSM_CK = SSM_CHUNK * SSM_GROUP_DIM

TOKEN_TILE = 512
VMEM_LIMIT = 56 * 1024 * 1024


def _cparams(n_axes):
    return pltpu.CompilerParams(dimension_semantics=("arbitrary",) * n_axes, vmem_limit_bytes=VMEM_LIMIT)


def _resident(block_shape, index_map):
    return pl.BlockSpec(block_shape, index_map, pipeline_mode=pl.Buffered(1))


def _rms(x, g):
    ms = jnp.mean(x * x, axis=-1, keepdims=True)
    return x * lax.rsqrt(ms + RMS_EPS) * g


def _inproj_kernel(x_ref, g_ref, w_ref, ufn_ref, q_ref, k_ref, v_ref, ussm_ref, upack_ref, gate_ref):
    h = _rms(x_ref[...], g_ref[...]).astype(BF16)

    def seg(lo, width):
        return jnp.dot(h, w_ref[:, lo:lo + width], preferred_element_type=F32)

    ufn_ref[...] = seg(0, FN_WIDTH)
    q_ref[...] = (seg(FN_WIDTH, NA_WIDTH) * (NA_HEAD_DIM ** -0.5)).astype(BF16)
    k_ref[...] = seg(FN_WIDTH + NA_WIDTH, NA_WIDTH).astype(BF16)
    v_ref[...] = seg(FN_WIDTH + 2 * NA_WIDTH, NA_WIDTH).astype(BF16)
    ussm = seg(FN_WIDTH + 3 * NA_WIDTH, SSM_WIDTH)
    ussm_ref[...] = ussm
    u3 = ussm.astype(BF16).reshape(TOKEN_TILE // SSM_CHUNK, SSM_CHUNK, SSM_WIDTH)
    for t in range(SSM_CHUNK):
        ut = u3[:, t, :]
        for g in range(SSM_GROUPS):
            upack_ref[g, :, t * SSM_GROUP_DIM:(t + 1) * SSM_GROUP_DIM] = (
                ut[:, g * SSM_GROUP_DIM:(g + 1) * SSM_GROUP_DIM])
    gate_lo = FN_WIDTH + 3 * NA_WIDTH + SSM_WIDTH
    for j in range(N_BRANCH * D_MODEL // 512):
        gate_ref[:, j * 512:(j + 1) * 512] = jax.nn.sigmoid(seg(gate_lo + j * 512, 512)).astype(BF16)


def _inproj(x, g_mix, w_in, layer):
    n = x.shape[0]
    tm = TOKEN_TILE
    row = lambda width: pl.BlockSpec((tm, width), lambda i: (i, 0))
    return pl.pallas_call(
        _inproj_kernel,
        grid=(n // tm,),
        in_specs=[row(D_MODEL),
                  _resident((None, 1, D_MODEL), lambda i: (layer, 0, 0)),
                  _resident((None, D_MODEL, D_IN), lambda i: (layer, 0, 0))],
        out_specs=[row(FN_WIDTH), row(NA_WIDTH), row(NA_WIDTH), row(NA_WIDTH), row(SSM_WIDTH),
                   pl.BlockSpec((SSM_GROUPS, tm // SSM_CHUNK, SSM_CK), lambda i: (0, i, 0)),
                   row(N_BRANCH * D_MODEL)],
        out_shape=[jax.ShapeDtypeStruct((n, FN_WIDTH), F32),
                   jax.ShapeDtypeStruct((n, NA_WIDTH), BF16),
                   jax.ShapeDtypeStruct((n, NA_WIDTH), BF16),
                   jax.ShapeDtypeStruct((n, NA_WIDTH), BF16),
                   jax.ShapeDtypeStruct((n, SSM_WIDTH), F32),
                   jax.ShapeDtypeStruct((SSM_GROUPS, n // SSM_CHUNK, SSM_CK), BF16),
                   jax.ShapeDtypeStruct((n, N_BRANCH * D_MODEL), BF16)],
        compiler_params=_cparams(1),
        name="inproj",
    )(x, g_mix, w_in)


def _dft_tables(seq):
    two_pi = 2.0 * math.pi
    k1 = lax.broadcasted_iota(jnp.int32, (DFT_N1, DFT_N1), 0)
    n1 = lax.broadcasted_iota(jnp.int32, (DFT_N1, DFT_N1), 1)
    ang = ((k1 * n1) % DFT_N1).astype(F32) * (two_pi / DFT_N1)
    wa = jnp.concatenate([jnp.cos(ang), -jnp.sin(ang)], axis=0)

    shape = (DFT_N1, DFT_N2, DFT_N2)
    kk1 = lax.broadcasted_iota(jnp.int32, shape, 0)
    kk2 = lax.broadcasted_iota(jnp.int32, shape, 1)
    nn2 = lax.broadcasted_iota(jnp.int32, shape, 2)
    ang = ((nn2 * (kk1 + DFT_N1 * kk2)) % seq).astype(F32) * (two_pi / seq)
    c, s = jnp.cos(ang), jnp.sin(ang)
    g = jnp.concatenate([jnp.concatenate([c, s], axis=2), jnp.concatenate([-s, c], axis=2)], axis=1)

    m = lax.broadcasted_iota(jnp.int32, (FN_WIDTH, FN_WIDTH), 0)
    j = lax.broadcasted_iota(jnp.int32, (FN_WIDTH, FN_WIDTH), 1)
    same_group = (m // FN_GROUP_DIM) == (j // FN_GROUP_DIM)
    ang = ((m * j) % FN_GROUP_DIM).astype(F32) * (two_pi / FN_GROUP_DIM)
    scale = 1.0 / math.sqrt(seq * FN_GROUP_DIM)
    cc = jnp.where(same_group, jnp.cos(ang) * scale, 0.0)
    sc = jnp.where(same_group, jnp.sin(ang) * scale, 0.0)
    return wa.astype(BF16), g.astype(BF16), cc.astype(BF16), sc.astype(BF16)


DFT_A_N2 = 8
DFT_C_K1 = 16


def _dft_a_kernel(w_ref, x_ref, o_ref):
    for j in range(DFT_A_N2):
        a = jnp.dot(w_ref[...], x_ref[:, j, :].astype(BF16), preferred_element_type=F32)
        o_ref[0, :, j, :] = a[:DFT_N1]
        o_ref[1, :, j, :] = a[DFT_N1:]


def _dft_c_kernel(ar_ref, ai_ref, g_ref, cc_ref, sc_ref, o_ref):
    for j in range(DFT_C_K1):
        a = jnp.concatenate([ar_ref[j], ai_ref[j]], axis=0).astype(BF16)
        z = jnp.dot(g_ref[j], a, preferred_element_type=F32)
        zr = z[:DFT_N2].astype(BF16)
        zi = z[DFT_N2:].astype(BF16)
        o = (jnp.dot(zr, cc_ref[...], preferred_element_type=F32)
             + jnp.dot(zi, sc_ref[...], preferred_element_type=F32))
        o_ref[:, j, :] = o.astype(BF16)


def _fourier_mix(ufn, batch, seq, tables):
    wa, g, cc, sc = tables
    x = ufn.reshape(batch, DFT_N1, DFT_N2, FN_WIDTH)
    nb = DFT_A_N2
    a = pl.pallas_call(
        _dft_a_kernel,
        grid=(batch, DFT_N2 // nb),
        in_specs=[_resident((2 * DFT_N1, DFT_N1), lambda b, j: (0, 0)),
                  pl.BlockSpec((None, DFT_N1, nb, FN_WIDTH), lambda b, j: (b, 0, j, 0))],
        out_specs=pl.BlockSpec((None, 2, DFT_N1, nb, FN_WIDTH), lambda b, j: (b, 0, 0, j, 0)),
        out_shape=jax.ShapeDtypeStruct((batch, 2, DFT_N1, DFT_N2, FN_WIDTH), F32),
        compiler_params=_cparams(2),
        name="dft_a",
    )(wa, x)
    kb = DFT_C_K1
    part = lambda p: pl.BlockSpec((None, None, kb, DFT_N2, FN_WIDTH), lambda b, i: (b, p, i, 0, 0))
    out = pl.pallas_call(
        _dft_c_kernel,
        grid=(batch, DFT_N1 // kb),
        in_specs=[part(0), part(1),
                  pl.BlockSpec((kb, 2 * DFT_N2, 2 * DFT_N2), lambda b, i: (i, 0, 0)),
                  _resident((FN_WIDTH, FN_WIDTH), lambda b, i: (0, 0)),
                  _resident((FN_WIDTH, FN_WIDTH), lambda b, i: (0, 0))],
        out_specs=pl.BlockSpec((None, DFT_N2, kb, FN_WIDTH), lambda b, i: (b, 0, i, 0)),
        out_shape=jax.ShapeDtypeStruct((batch, DFT_N2, DFT_N1, FN_WIDTH), BF16),
        compiler_params=_cparams(2),
        name="dft_c",
    )(a, a, g, cc, sc)
    return out.reshape(batch * seq, FN_WIDTH)


def _na_bias_table(rpb, rows):
    kr = min(NA_KR, rows)
    n_blocks = rows // NA_RQ
    n_dr = 2 * NA_KR - 1
    pad = GRID_W - NA_KC
    ext = jnp.pad(rpb, ((0, 0), (0, 0), (pad, pad)))
    slab = jnp.stack([ext[:, :, GRID_W - 1 - qc:2 * GRID_W - 1 - qc] for qc in range(GRID_W)], axis=2)
    qc, kc = np.arange(GRID_W)[:, None], np.arange(GRID_W)[None, :]
    col_start = np.clip(qc - NA_KC // 2, 0, GRID_W - NA_KC)
    col_ok = (kc >= col_start) & (kc < col_start + NA_KC)
    slab = jnp.where(col_ok[None, None], slab, NEG_INF)
    masked = jnp.full((NA_HEADS, 1, GRID_W, GRID_W), NEG_INF, F32)
    slab = jnp.concatenate([slab.astype(F32), masked], axis=1)
    variants = []
    for blk in (0, 1, n_blocks - 1):
        r = blk * NA_RQ + np.arange(NA_RQ)[:, None]
        key_row = (blk - 1) * NA_RQ + np.arange(3 * NA_RQ)[None, :]
        row_start = np.clip(r - kr // 2, 0, rows - kr)
        row_ok = (key_row >= row_start) & (key_row < row_start + kr)
        dr = np.where(row_ok, key_row - r + NA_KR - 1, n_dr)
        t = jnp.stack([jnp.concatenate([slab[:, int(d)] for d in dr_row], axis=-1) for dr_row in dr], axis=1)
        variants.append(t.reshape(NA_HEADS, NA_QB, 3 * NA_QB))
    return jnp.stack(variants, axis=0)


def _na_kernel(q_ref, kp_ref, kc_ref, kn_ref, vp_ref, vc_ref, vn_ref, bm_ref, o_ref):
    lane = lax.broadcasted_iota(jnp.int32, (1, 2 * NA_HEAD_DIM), 1)
    k_refs = (kp_ref, kc_ref, kn_ref)
    v_refs = (vp_ref, vc_ref, vn_ref)
    for hp in range(NA_HEADS // 2):
        sl = slice(2 * NA_HEAD_DIM * hp, 2 * NA_HEAD_DIM * (hp + 1))
        q2 = q_ref[:, sl]
        ks = [r[:, sl] for r in k_refs]
        vs = [r[:, sl] for r in v_refs]
        outs = []
        for e in range(2):
            in_head = (lane >= NA_HEAD_DIM * e) & (lane < NA_HEAD_DIM * (e + 1))
            qm = jnp.where(in_head, q2, jnp.zeros_like(q2))
            s = [lax.dot_general(qm, kk, (((1,), (1,)), ((), ())), preferred_element_type=F32)
                 + bm_ref[2 * hp + e, :, NA_QB * i:NA_QB * (i + 1)]
                 for i, kk in enumerate(ks)]
            m = jnp.maximum(jnp.maximum(jnp.max(s[0], axis=-1, keepdims=True),
                                        jnp.max(s[1], axis=-1, keepdims=True)),
                            jnp.max(s[2], axis=-1, keepdims=True))
            p = [jnp.exp(si - m) for si in s]
            denom = (jnp.sum(p[0], axis=-1, keepdims=True) + jnp.sum(p[1], axis=-1, keepdims=True)
                     + jnp.sum(p[2], axis=-1, keepdims=True))
            pv = (jnp.dot(p[0].astype(BF16), vs[0], preferred_element_type=F32)
                  + jnp.dot(p[1].astype(BF16), vs[1], preferred_element_type=F32)
                  + jnp.dot(p[2].astype(BF16), vs[2], preferred_element_type=F32))
            outs.append(pv / denom)
        o_ref[:, sl] = jnp.where(lane < NA_HEAD_DIM, outs[0], outs[1]).astype(BF16)


def _neighbourhood_attention(q, k, v, bias_table, batch, seq):
    n_blocks = seq // NA_QB
    q3, k3, v3 = (t.reshape(batch, seq, NA_WIDTH) for t in (q, k, v))
    blk = lambda shift: pl.BlockSpec(
        (None, NA_QB, NA_WIDTH), lambda b, i: (b, jnp.clip(i + shift, 0, n_blocks - 1), 0))
    variant = lambda b, i: (jnp.where(i == 0, 0, jnp.where(i == n_blocks - 1, 2, 1)), 0, 0, 0)
    out = pl.pallas_call(
        _na_kernel,
        grid=(batch, n_blocks),
        in_specs=[blk(0), blk(-1), blk(0), blk(1), blk(-1), blk(0), blk(1),
                  pl.BlockSpec((None, NA_HEADS, NA_QB, 3 * NA_QB), variant)],
        out_specs=blk(0),
        out_shape=jax.ShapeDtypeStruct((batch, seq, NA_WIDTH), BF16),
        compiler_params=_cparams(2),
        name="natten",
    )(q3, k3, k3, k3, v3, v3, v3, bias_table)
    return out.reshape(batch * seq, NA_WIDTH)


def _cmul(ar, ai, br, bi):
    return ar * br - ai * bi, ar * bi + ai * br


def _ssm_tables(a_re, a_im, log_dt, b_re, b_im, c_re, c_im):
    hi = lax.Precision.HIGHEST
    dt = jnp.exp(log_dt)[..., None]
    zr, zi = a_re * dt, a_im * dt
    taus = jnp.arange(SSM_CHUNK + 1, dtype=F32)[:, None, None, None]
    mag = jnp.exp(zr[None] * taus)
    pr, pi = mag * jnp.cos(zi[None] * taus), mag * jnp.sin(zi[None] * taus)
    lbr, lbi = pr[1], pi[1]
    den = a_re * a_re + a_im * a_im
    fr = ((lbr - 1.0) * a_re + lbi * a_im) / den
    fi = (lbi * a_re - (lbr - 1.0) * a_im) / den
    bbr, bbi = _cmul(fr[..., None], fi[..., None], b_re, b_im)

    wr, wi = _cmul(c_re[:, None], c_im[:, None],
                   pr[:SSM_CHUNK].transpose(1, 0, 2, 3)[:, :, :, None, :],
                   pi[:SSM_CHUNK].transpose(1, 0, 2, 3)[:, :, :, None, :])
    kern = (jnp.einsum('dtgop,dgpc->dtgoc', wr, bbr, precision=hi)
            - jnp.einsum('dtgop,dgpc->dtgoc', wi, bbi, precision=hi))
    kf, kb = kern[0], kern[1]
    kmix = jnp.concatenate([kb[:0:-1], (kf[0] + kb[0])[None], kf[1:]], axis=0)
    t_idx = np.arange(SSM_CHUNK)
    delta = t_idx[None, :] - t_idx[:, None] + SSM_CHUNK - 1
    intra = kmix[delta]
    intra = intra.transpose(2, 0, 4, 1, 3).reshape(SSM_GROUPS, SSM_CK, SSM_CK)

    pf_r, pf_i = _cmul(pr[SSM_CHUNK - 1::-1, 0][..., None], pi[SSM_CHUNK - 1::-1, 0][..., None],
                       bbr[0][None], bbi[0][None])
    pb_r, pb_i = _cmul(pr[:SSM_CHUNK, 1][..., None], pi[:SSM_CHUNK, 1][..., None], bbr[1][None], bbi[1][None])
    sp = jnp.stack([pf_r, pf_i, pb_r, pb_i], axis=0)
    sp = sp.transpose(2, 1, 4, 0, 3).reshape(SSM_GROUPS, SSM_CK, 4, SSM_STATE)
    sp = sp.reshape(SSM_PAIRS, 2, SSM_CK, 4, 1, SSM_STATE)
    eye = jnp.eye(2, dtype=F32).reshape(1, 2, 1, 1, 2, 1)
    sproj = (sp * eye).reshape(SSM_PAIRS, 2, SSM_CK, 4 * 2 * SSM_STATE)

    qf_r, qf_i = _cmul(c_re[0][None], c_im[0][None],
                       pr[1:, 0][:, :, None, :], pi[1:, 0][:, :, None, :])
    qb_r, qb_i = _cmul(c_re[1][None], c_im[1][None],
                       pr[:0:-1, 1][:, :, None, :], pi[:0:-1, 1][:, :, None, :])
    qp = jnp.stack([qf_r, -qf_i, qb_r, -qb_i], axis=0)
    qp = qp.transpose(2, 0, 4, 1, 3).reshape(SSM_PAIRS, 2, 4, SSM_STATE, 1, SSM_CK)
    eye = jnp.eye(2, dtype=F32).reshape(1, 2, 1, 1, 2, 1)
    qproj = (qp * eye).transpose(0, 2, 1, 3, 4, 5).reshape(SSM_PAIRS, 4, 2 * SSM_STATE, 2 * SSM_CK)

    steps = (jnp.arange(1, 9, dtype=F32) * SSM_CHUNK)[:, None, None, None]
    mag = jnp.exp(zr[None] * steps)
    sr, si = mag * jnp.cos(zi[None] * steps), mag * jnp.sin(zi[None] * steps)
    n_state = SSM_GROUPS * SSM_STATE
    lam_pows = jnp.stack([jnp.stack([sr[:, 0], si[:, 0]]), jnp.stack([sr[::-1, 1], si[::-1, 1]])])
    lam_pows = lam_pows.reshape(4 * 8, n_state)
    return intra.astype(BF16), sproj.astype(BF16), qproj.astype(BF16), lam_pows


SSM_ROW_TILE = 1024
SSM_SCAN_LANES = 128


def _ssm_state_kernel(u_ref, sp_ref, st_ref):
    st = (jnp.dot(u_ref[0], sp_ref[0], preferred_element_type=F32)
          + jnp.dot(u_ref[1], sp_ref[1], preferred_element_type=F32))
    for part in range(4):
        st_ref[part] = st[:, 128 * part:128 * (part + 1)]


def _ssm_scan_kernel(st_ref, lam_ref, xin_ref):
    n_seq, n_tiles, lanes = st_ref.shape[1], st_ref.shape[2] // 8, st_ref.shape[3]
    row = lax.broadcasted_iota(jnp.int32, (8, lanes), 0)

    def bcast(x, r):
        return jnp.broadcast_to(x[r:r + 1, :], (8, lanes))

    tabs = []
    for d in range(2):
        pr, pi = lam_ref[16 * d:16 * d + 8, :], lam_ref[16 * d + 8:16 * d + 16, :]
        at = (lambda m: m - 1) if d == 0 else (lambda m: 8 - m)
        tabs.append((pr, pi, [(bcast(pr, at(m)), bcast(pi, at(m))) for m in (1, 2, 4)]))

    def shifted(x, dist, forward):
        if forward:
            return jnp.where(row >= dist, pltpu.roll(x, dist, 0), 0.0)
        return jnp.where(row < 8 - dist, pltpu.roll(x, 8 - dist, 0), 0.0)

    def scan_tile(sr, si, cr, ci, d):
        forward = d == 0
        pr, pi, steps = tabs[d]
        hr, hi = sr, si
        for (lr, li), dist in zip(steps, (1, 2, 4)):
            mr, mi = _cmul(lr, li, shifted(hr, dist, forward), shifted(hi, dist, forward))
            hr, hi = hr + mr, hi + mi
        mr, mi = _cmul(pr, pi, cr, ci)
        er, ei = hr + mr, hi + mi
        if forward:
            xr = jnp.where(row >= 1, pltpu.roll(er, 1, 0), cr)
            xi = jnp.where(row >= 1, pltpu.roll(ei, 1, 0), ci)
            return xr, xi, bcast(er, 7), bcast(ei, 7)
        xr = jnp.where(row < 7, pltpu.roll(er, 7, 0), cr)
        xi = jnp.where(row < 7, pltpu.roll(ei, 7, 0), ci)
        return xr, xi, bcast(er, 0), bcast(ei, 0)

    def body(i, carry):
        new = []
        for s in range(n_seq):
            for d in range(2):
                cr, ci = carry[2 * (2 * s + d)], carry[2 * (2 * s + d) + 1]
                r0 = pl.multiple_of((i if d == 0 else n_tiles - 1 - i) * 8, 8)
                sr, si = st_ref[2 * d, s, pl.ds(r0, 8), :], st_ref[2 * d + 1, s, pl.ds(r0, 8), :]
                xr, xi, ncr, nci = scan_tile(sr, si, cr, ci, d)
                xin_ref[2 * d, s, pl.ds(r0, 8), :] = xr
                xin_ref[2 * d + 1, s, pl.ds(r0, 8), :] = xi
                new += [ncr, nci]
        return tuple(new)

    zero = jnp.zeros((8, lanes), F32)
    lax.fori_loop(0, n_tiles, body, (zero,) * (4 * n_seq))


def _ssm_out_kernel(u_ref, xin_ref, mi_ref, qp_ref, o_ref):
    acc = jnp.dot(xin_ref[0].astype(BF16), qp_ref[0], preferred_element_type=F32)
    for part in range(1, 4):
        acc = acc + jnp.dot(xin_ref[part].astype(BF16), qp_ref[part], preferred_element_type=F32)
    for e in range(2):
        y0 = jnp.dot(u_ref[e], mi_ref[e], preferred_element_type=F32)
        o_ref[e] = y0 + acc[:, SSM_CK * e:SSM_CK * (e + 1)]


def _ssm_scan(upack, batch, seq, tables):
    intra, sproj, qproj, lam_pows = tables
    n_chunks = seq // SSM_CHUNK
    rows = n_chunks * batch
    n_state = SSM_GROUPS * SSM_STATE
    tr = SSM_ROW_TILE
    st = pl.pallas_call(
        _ssm_state_kernel,
        grid=(SSM_PAIRS, rows // tr),
        in_specs=[pl.BlockSpec((2, tr, SSM_CK), lambda p, r: (p, r, 0)),
                  pl.BlockSpec((None, 2, SSM_CK, 512), lambda p, r: (p, 0, 0, 0))],
        out_specs=pl.BlockSpec((4, tr, 128), lambda p, r: (0, r, p)),
        out_shape=jax.ShapeDtypeStruct((4, rows, n_state), F32),
        compiler_params=_cparams(2),
        name="ssm_state",
    )(upack, sproj)
    lanes = SSM_SCAN_LANES
    seq_block = pl.BlockSpec((4, batch, n_chunks, lanes), lambda j: (0, 0, 0, j))
    xin = pl.pallas_call(
        _ssm_scan_kernel,
        grid=(n_state // lanes,),
        in_specs=[seq_block, pl.BlockSpec((32, lanes), lambda j: (0, j))],
        out_specs=seq_block,
        out_shape=jax.ShapeDtypeStruct((4, batch, n_chunks, n_state), F32),
        compiler_params=_cparams(1),
        name="ssm_scan",
    )(st.reshape(4, batch, n_chunks, n_state), lam_pows)
    return pl.pallas_call(
        _ssm_out_kernel,
        grid=(SSM_PAIRS, rows // tr),
        in_specs=[pl.BlockSpec((2, tr, SSM_CK), lambda p, r: (p, r, 0)),
                  pl.BlockSpec((4, tr, 128), lambda p, r: (0, r, p)),
                  pl.BlockSpec((2, SSM_CK, SSM_CK), lambda p, r: (p, 0, 0)),
                  pl.BlockSpec((None, 4, 128, 2 * SSM_CK), lambda p, r: (p, 0, 0, 0))],
        out_specs=pl.BlockSpec((2, tr, SSM_CK), lambda p, r: (p, r, 0)),
        out_shape=jax.ShapeDtypeStruct((SSM_GROUPS, rows, SSM_CK), F32),
        compiler_params=_cparams(2),
        name="ssm_out",
    )(upack, xin.reshape(4, rows, n_state), intra, qproj)


def _merge_kernel(x_ref, gate_ref, fm_ref, na_ref, ypack_ref, ussm_ref, dskip_ref, wglu_ref,
                  wfn_ref, wna_ref, wssm_ref, wout_ref, o_ref, ytok_ref, yrow_ref):
    for t in range(SSM_CHUNK):
        for g in range(SSM_GROUPS):
            yrow_ref[:, g * SSM_GROUP_DIM:(g + 1) * SSM_GROUP_DIM] = (
                ypack_ref[g, :, t * SSM_GROUP_DIM:(t + 1) * SSM_GROUP_DIM])
        ytok_ref[:, t, :] = yrow_ref[...]
    yscan = ytok_ref[...].reshape(TOKEN_TILE, SSM_WIDTH)
    y = yscan + dskip_ref[...] * ussm_ref[...]
    y = jax.nn.gelu(y)
    y = y * jax.nn.sigmoid(jnp.dot(y.astype(BF16), wglu_ref[...], preferred_element_type=F32))
    y_ssm = jnp.dot(y.astype(BF16), wssm_ref[...], preferred_element_type=F32)
    y_fn = jnp.dot(fm_ref[...], wfn_ref[...], preferred_element_type=F32)
    y_na = jnp.dot(na_ref[...], wna_ref[...], preferred_element_type=F32)
    merged = (gate_ref[:, :D_MODEL].astype(F32) * y_fn
              + gate_ref[:, D_MODEL:2 * D_MODEL].astype(F32) * y_na
              + gate_ref[:, 2 * D_MODEL:].astype(F32) * y_ssm)
    o_ref[...] = x_ref[...] + jnp.dot(merged.astype(BF16), wout_ref[...], preferred_element_type=F32)


def _merge(x, gates, fm, na, ypack, ussm, ssm_d, w_glu, w_br_fn, w_br_na, w_br_ssm, w_out, layer):
    n = x.shape[0]
    tm = TOKEN_TILE
    row = lambda width: pl.BlockSpec((tm, width), lambda i: (i, 0))
    wspec = lambda k, m: _resident((None, k, m), lambda i: (layer, 0, 0))
    return pl.pallas_call(
        _merge_kernel,
        grid=(n // tm,),
        in_specs=[row(D_MODEL), row(N_BRANCH * D_MODEL), row(FN_WIDTH), row(NA_WIDTH),
                  pl.BlockSpec((SSM_GROUPS, tm // SSM_CHUNK, SSM_CK), lambda i: (0, i, 0)),
                  row(SSM_WIDTH), wspec(1, SSM_WIDTH), wspec(SSM_WIDTH, SSM_WIDTH),
                  wspec(FN_WIDTH, D_MODEL), wspec(NA_WIDTH, D_MODEL), wspec(SSM_WIDTH, D_MODEL),
                  wspec(D_MODEL, D_MODEL)],
        out_specs=row(D_MODEL),
        out_shape=jax.ShapeDtypeStruct((n, D_MODEL), F32),
        scratch_shapes=[pltpu.VMEM((tm // SSM_CHUNK, SSM_CHUNK, SSM_WIDTH), F32),
                        pltpu.VMEM((tm // SSM_CHUNK, SSM_WIDTH), F32)],
        compiler_params=_cparams(1),
        name="merge",
    )(x, gates, fm, na, ypack, ussm, ssm_d, w_glu, w_br_fn, w_br_na, w_br_ssm, w_out)


def _ffn_kernel(x_ref, g_ref, wup_ref, wdown_ref, gfin_ref, o_ref, up_ref, *, final_norm):
    x = x_ref[...]
    h = _rms(x, g_ref[...]).astype(BF16)
    for c in range(D_FF // D_MODEL):
        cols = slice(c * D_MODEL, (c + 1) * D_MODEL)
        a = jnp.maximum(jnp.dot(h, wup_ref[:, cols], preferred_element_type=F32), 0.0)
        up_ref[:, cols] = (a * a).astype(BF16)
    y = x + jnp.dot(up_ref[...], wdown_ref[...], preferred_element_type=F32)
    if final_norm:
        y = _rms(y, gfin_ref[...])
    o_ref[...] = y


def _ffn(x, g_ffn, w_up, w_down, g_final, layer, final_norm):
    n = x.shape[0]
    tm = TOKEN_TILE
    row = pl.BlockSpec((tm, D_MODEL), lambda i: (i, 0))
    return pl.pallas_call(
        functools.partial(_ffn_kernel, final_norm=final_norm),
        grid=(n // tm,),
        in_specs=[row,
                  _resident((None, 1, D_MODEL), lambda i: (layer, 0, 0)),
                  _resident((None, D_MODEL, D_FF), lambda i: (layer, 0, 0)),
                  _resident((None, D_FF, D_MODEL), lambda i: (layer, 0, 0)),
                  _resident((1, D_MODEL), lambda i: (0, 0))],
        out_specs=row,
        out_shape=jax.ShapeDtypeStruct((n, D_MODEL), F32),
        scratch_shapes=[pltpu.VMEM((tm, D_FF), BF16)],
        compiler_params=_cparams(1),
        name="ffn",
    )(x, g_ffn, w_up, w_down, g_final)


def kernel(x, g_mix, w_in, na_rpb, ssm_a_re, ssm_a_im, ssm_log_dt, ssm_b_re, ssm_b_im, ssm_c_re, ssm_c_im,
           ssm_d, w_glu, w_br_fn, w_br_na, w_br_ssm, w_out, g_ffn, w_up, w_down, g_final):
    batch, seq, _ = x.shape
    depth = w_in.shape[0]
    assert seq == DFT_N1 * DFT_N2 and seq % (GRID_W * NA_RQ) == 0 and (batch * seq) % TOKEN_TILE == 0
    rows = seq // GRID_W

    to_bf16 = lambda w: w.astype(BF16)
    w_in, w_glu, w_br_fn, w_br_na, w_br_ssm, w_out, w_up, w_down = map(
        to_bf16, (w_in, w_glu, w_br_fn, w_br_na, w_br_ssm, w_out, w_up, w_down))
    g_mix3 = g_mix.reshape(depth, 1, D_MODEL)
    g_ffn3 = g_ffn.reshape(depth, 1, D_MODEL)
    ssm_d3 = ssm_d.reshape(depth, 1, SSM_WIDTH)
    g_final2 = g_final.reshape(1, D_MODEL)
    dft_tables = _dft_tables(seq)

    xs = x.reshape(batch * seq, D_MODEL)
    for l in range(depth):
        ufn, q, k, v, ussm, upack, gates = _inproj(xs, g_mix3, w_in, l)
        fm = _fourier_mix(ufn, batch, seq, dft_tables)
        na = _neighbourhood_attention(q, k, v, _na_bias_table(na_rpb[l], rows), batch, seq)
        ypack = _ssm_scan(upack, batch, seq,
                          _ssm_tables(ssm_a_re[l], ssm_a_im[l], ssm_log_dt[l], ssm_b_re[l], ssm_b_im[l],
                                      ssm_c_re[l], ssm_c_im[l]))
        xs = _merge(xs, gates, fm, na, ypack, ussm, ssm_d3, w_glu, w_br_fn, w_br_na, w_br_ssm, w_out, l)
        xs = _ffn(xs, g_ffn3, w_up, w_down, g_final2, l, final_norm=(l == depth - 1))
    return xs.reshape(batch, seq, D_MODEL)
```

```python
import functools
import math

import numpy as np
import jax
import jax.numpy as jnp
from jax import lax
from jax.experimental import pallas as pl
from jax.experimental.pallas import tpu as pltpu

F32 = jnp.float32
BF16 = jnp.bfloat16

D_MODEL = 1024
GRID_W = 64
FN_GROUP_DIM = 64
FN_WIDTH = 256
NA_HEADS = 8
NA_HEAD_DIM = 64
NA_WIDTH = 512
NA_KR = 8
NA_KC = 16
NEG_INF = -1e30
SSM_GROUPS = 16
SSM_GROUP_DIM = 16
SSM_WIDTH = 256
SSM_STATE = 64
N_BRANCH = 3
D_FF = 4 * D_MODEL
D_IN = FN_WIDTH + 3 * NA_WIDTH + SSM_WIDTH + N_BRANCH * D_MODEL
RMS_EPS = 1e-6
LOG2_E = 1.0 / math.log(2.0)

DFT_N1 = 128
DFT_N2 = 64
NA_RQ = 4
NA_QB = NA_RQ * GRID_W
SSM_CHUNK = 16
SSM_PAIRS = SSM_GROUPS // 2
SSM_CK = SSM_CHUNK * SSM_GROUP_DIM

TOKEN_TILE = 512
VMEM_LIMIT = 56 * 1024 * 1024


def _cparams(n_axes):
    return pltpu.CompilerParams(dimension_semantics=("arbitrary",) * n_axes, vmem_limit_bytes=VMEM_LIMIT)


def _resident(block_shape, index_map):
    return pl.BlockSpec(block_shape, index_map, pipeline_mode=pl.Buffered(1))


def _rms(x, g):
    ms = jnp.mean(x * x, axis=-1, keepdims=True)
    return x * lax.rsqrt(ms + RMS_EPS) * g


def _inproj_kernel(x_ref, g_ref, w_ref, ufn_ref, q_ref, k_ref, v_ref, ussm_ref, upack_ref, gate_ref):
    h = _rms(x_ref[...], g_ref[...]).astype(BF16)

    def seg(lo, width):
        return jnp.dot(h, w_ref[:, lo:lo + width], preferred_element_type=F32)

    ufn_ref[...] = seg(0, FN_WIDTH)
    q_ref[...] = (seg(FN_WIDTH, NA_WIDTH) * (NA_HEAD_DIM ** -0.5 * LOG2_E)).astype(BF16)
    k_ref[...] = seg(FN_WIDTH + NA_WIDTH, NA_WIDTH).astype(BF16)
    v_ref[...] = seg(FN_WIDTH + 2 * NA_WIDTH, NA_WIDTH).astype(BF16)
    ussm = seg(FN_WIDTH + 3 * NA_WIDTH, SSM_WIDTH)
    ussm_ref[...] = ussm
    u3 = ussm.astype(BF16).reshape(TOKEN_TILE // SSM_CHUNK, SSM_CHUNK, SSM_WIDTH)
    for t in range(SSM_CHUNK):
        ut = u3[:, t, :]
        for g in range(SSM_GROUPS):
            upack_ref[g, :, t * SSM_GROUP_DIM:(t + 1) * SSM_GROUP_DIM] = (
                ut[:, g * SSM_GROUP_DIM:(g + 1) * SSM_GROUP_DIM])
    gate_lo = FN_WIDTH + 3 * NA_WIDTH + SSM_WIDTH
    for j in range(N_BRANCH):
        gate_ref[:, j * D_MODEL:(j + 1) * D_MODEL] = jax.nn.sigmoid(seg(gate_lo + j * D_MODEL, D_MODEL)).astype(BF16)


def _inproj(x, g_mix, w_in, layer):
    n = x.shape[0]
    tm = TOKEN_TILE
    row = lambda width: pl.BlockSpec((tm, width), lambda i: (i, 0))
    return pl.pallas_call(
        _inproj_kernel,
        grid=(n // tm,),
        in_specs=[row(D_MODEL),
                  _resident((None, 1, D_MODEL), lambda i: (layer, 0, 0)),
                  _resident((None, D_MODEL, D_IN), lambda i: (layer, 0, 0))],
        out_specs=[row(FN_WIDTH), row(NA_WIDTH), row(NA_WIDTH), row(NA_WIDTH), row(SSM_WIDTH),
                   pl.BlockSpec((SSM_GROUPS, tm // SSM_CHUNK, SSM_CK), lambda i: (0, i, 0)),
                   row(N_BRANCH * D_MODEL)],
        out_shape=[jax.ShapeDtypeStruct((n, FN_WIDTH), F32),
                   jax.ShapeDtypeStruct((n, NA_WIDTH), BF16),
                   jax.ShapeDtypeStruct((n, NA_WIDTH), BF16),
                   jax.ShapeDtypeStruct((n, NA_WIDTH), BF16),
                   jax.ShapeDtypeStruct((n, SSM_WIDTH), F32),
                   jax.ShapeDtypeStruct((SSM_GROUPS, n // SSM_CHUNK, SSM_CK), BF16),
                   jax.ShapeDtypeStruct((n, N_BRANCH * D_MODEL), BF16)],
        compiler_params=_cparams(1),
        name="inproj",
    )(x, g_mix, w_in)


def _dft_tables(seq):
    two_pi = 2.0 * math.pi
    k1 = lax.broadcasted_iota(jnp.int32, (DFT_N1, DFT_N1), 0)
    n1 = lax.broadcasted_iota(jnp.int32, (DFT_N1, DFT_N1), 1)
    ang = ((k1 * n1) % DFT_N1).astype(F32) * (two_pi / DFT_N1)
    wa = jnp.concatenate([jnp.cos(ang), -jnp.sin(ang)], axis=0)

    shape = (DFT_N1, DFT_N2, DFT_N2)
    kk1 = lax.broadcasted_iota(jnp.int32, shape, 0)
    kk2 = lax.broadcasted_iota(jnp.int32, shape, 1)
    nn2 = lax.broadcasted_iota(jnp.int32, shape, 2)
    ang = ((nn2 * (kk1 + DFT_N1 * kk2)) % seq).astype(F32) * (two_pi / seq)
    c, s = jnp.cos(ang), jnp.sin(ang)
    g = jnp.concatenate([jnp.concatenate([c, s], axis=2), jnp.concatenate([-s, c], axis=2)], axis=1)

    m = lax.broadcasted_iota(jnp.int32, (FN_WIDTH, FN_WIDTH), 0)
    j = lax.broadcasted_iota(jnp.int32, (FN_WIDTH, FN_WIDTH), 1)
    same_group = (m // FN_GROUP_DIM) == (j // FN_GROUP_DIM)
    ang = ((m * j) % FN_GROUP_DIM).astype(F32) * (two_pi / FN_GROUP_DIM)
    scale = 1.0 / math.sqrt(seq * FN_GROUP_DIM)
    cc = jnp.where(same_group, jnp.cos(ang) * scale, 0.0)
    sc = jnp.where(same_group, jnp.sin(ang) * scale, 0.0)
    return wa.astype(BF16), g.astype(BF16), cc.astype(BF16), sc.astype(BF16)


DFT_A_N2 = 8
DFT_C_K1 = 16


def _dft_a_kernel(w_ref, x_ref, o_ref):
    for j in range(DFT_A_N2):
        a = jnp.dot(w_ref[...], x_ref[:, j, :].astype(BF16), preferred_element_type=F32)
        o_ref[0, :, j, :] = a[:DFT_N1]
        o_ref[1, :, j, :] = a[DFT_N1:]


def _dft_c_kernel(ar_ref, ai_ref, g_ref, cc_ref, sc_ref, o_ref, zr_ref, zi_ref):
    for j in range(DFT_C_K1):
        a = jnp.concatenate([ar_ref[j], ai_ref[j]], axis=0).astype(BF16)
        z = jnp.dot(g_ref[j], a, preferred_element_type=F32)
        zr_ref[j * DFT_N2:(j + 1) * DFT_N2, :] = z[:DFT_N2].astype(BF16)
        zi_ref[j * DFT_N2:(j + 1) * DFT_N2, :] = z[DFT_N2:].astype(BF16)
    o = (jnp.dot(zr_ref[...], cc_ref[...], preferred_element_type=F32)
         + jnp.dot(zi_ref[...], sc_ref[...], preferred_element_type=F32)).astype(BF16)
    for j in range(DFT_C_K1):
        o_ref[:, j, :] = o[j * DFT_N2:(j + 1) * DFT_N2]


def _fourier_mix(ufn, batch, seq, tables):
    wa, g, cc, sc = tables
    x = ufn.reshape(batch, DFT_N1, DFT_N2, FN_WIDTH)
    nb = DFT_A_N2
    a = pl.pallas_call(
        _dft_a_kernel,
        grid=(batch, DFT_N2 // nb),
        in_specs=[_resident((2 * DFT_N1, DFT_N1), lambda b, j: (0, 0)),
                  pl.BlockSpec((None, DFT_N1, nb, FN_WIDTH), lambda b, j: (b, 0, j, 0))],
        out_specs=pl.BlockSpec((None, 2, DFT_N1, nb, FN_WIDTH), lambda b, j: (b, 0, 0, j, 0)),
        out_shape=jax.ShapeDtypeStruct((batch, 2, DFT_N1, DFT_N2, FN_WIDTH), F32),
        compiler_params=_cparams(2),
        name="dft_a",
    )(wa, x)
    kb = DFT_C_K1
    part = lambda p: pl.BlockSpec((None, None, kb, DFT_N2, FN_WIDTH), lambda b, i: (b, p, i, 0, 0))
    out = pl.pallas_call(
        _dft_c_kernel,
        grid=(batch, DFT_N1 // kb),
        in_specs=[part(0), part(1),
                  pl.BlockSpec((kb, 2 * DFT_N2, 2 * DFT_N2), lambda b, i: (i, 0, 0)),
                  _resident((FN_WIDTH, FN_WIDTH), lambda b, i: (0, 0)),
                  _resident((FN_WIDTH, FN_WIDTH), lambda b, i: (0, 0))],
        out_specs=pl.BlockSpec((None, DFT_N2, kb, FN_WIDTH), lambda b, i: (b, 0, i, 0)),
        out_shape=jax.ShapeDtypeStruct((batch, DFT_N2, DFT_N1, FN_WIDTH), BF16),
        scratch_shapes=[pltpu.VMEM((kb * DFT_N2, FN_WIDTH), BF16), pltpu.VMEM((kb * DFT_N2, FN_WIDTH), BF16)],
        compiler_params=_cparams(2),
        name="dft_c",
    )(a, a, g, cc, sc)
    return out.reshape(batch * seq, FN_WIDTH)


def _na_bias_tables(rpb, rows):
    depth = rpb.shape[0]
    kr = min(NA_KR, rows)
    n_blocks = rows // NA_RQ
    n_dr = 2 * NA_KR - 1
    pad = GRID_W - NA_KC
    period = 2 * GRID_W - 1
    ext = jnp.pad(rpb * LOG2_E, ((0, 0), (0, 0), (0, 0), (pad, pad)))
    flat = jnp.tile(ext, (1, 1, 1, GRID_W))[..., GRID_W - 1:GRID_W - 1 + GRID_W * (period - 1)]
    slab = flat.reshape(depth, NA_HEADS, n_dr, GRID_W, period - 1)[..., :GRID_W]
    qc, kc = np.arange(GRID_W)[:, None], np.arange(GRID_W)[None, :]
    col_start = np.clip(qc - NA_KC // 2, 0, GRID_W - NA_KC)
    col_ok = (kc >= col_start) & (kc < col_start + NA_KC)
    slab = jnp.where(col_ok, slab, NEG_INF)
    masked = jnp.full((depth, NA_HEADS, 1, GRID_W, GRID_W), NEG_INF, F32)
    slab = jnp.concatenate([slab, masked], axis=2)
    variants = []
    for blk in (0, 1, n_blocks - 1):
        r = blk * NA_RQ + np.arange(NA_RQ)[:, None]
        key_row = (blk - 1) * NA_RQ + np.arange(3 * NA_RQ)[None, :]
        row_start = np.clip(r - kr // 2, 0, rows - kr)
        row_ok = (key_row >= row_start) & (key_row < row_start + kr)
        dr = np.where(row_ok, key_row - r + NA_KR - 1, n_dr)
        t = jnp.take(slab, dr.reshape(-1), axis=2)
        t = t.reshape(depth, NA_HEADS, NA_RQ, 3 * NA_RQ, GRID_W, GRID_W).transpose(0, 1, 2, 4, 3, 5)
        variants.append(t.reshape(depth, NA_HEADS, NA_QB, 3 * NA_QB))
    return jnp.stack(variants, axis=1)


def _na_kernel(q_ref, kp_ref, kc_ref, kn_ref, vp_ref, vc_ref, vn_ref, bm_ref, o_ref):
    lane = lax.broadcasted_iota(jnp.int32, (1, 2 * NA_HEAD_DIM), 1)
    first = lane < NA_HEAD_DIM
    k_refs = (kp_ref, kc_ref, kn_ref)
    v_refs = (vp_ref, vc_ref, vn_ref)
    for hp in range(NA_HEADS // 2):
        sl = slice(2 * NA_HEAD_DIM * hp, 2 * NA_HEAD_DIM * (hp + 1))
        q2 = q_ref[:, sl]
        acc = []
        for e in range(2):
            mine = first if e == 0 else jnp.logical_not(first)
            qm = jnp.where(mine, q2, jnp.zeros_like(q2))
            ms, rs = [], []
            for i in range(3):
                s = lax.dot_general(qm, k_refs[i][:, sl], (((1,), (1,)), ((), ())), preferred_element_type=F32)
                s = s + bm_ref[2 * hp + e, :, NA_QB * i:NA_QB * (i + 1)]
                m = jnp.max(s, axis=-1, keepdims=True)
                p = jnp.exp2(s - m).astype(BF16)
                v2 = v_refs[i][:, sl]
                vm = jnp.where(mine, v2, jnp.ones_like(v2))
                ms.append(m)
                rs.append(jnp.dot(p, vm, preferred_element_type=F32))
            m_all = jnp.maximum(jnp.maximum(ms[0], ms[1]), ms[2])
            acc.append(rs[0] * jnp.exp2(ms[0] - m_all) + rs[1] * jnp.exp2(ms[1] - m_all)
                       + rs[2] * jnp.exp2(ms[2] - m_all))
        num = jnp.where(first, acc[0], acc[1])
        den = pltpu.roll(jnp.where(first, acc[1], acc[0]), NA_HEAD_DIM, 1)
        o_ref[:, sl] = (num / den).astype(BF16)


def _neighbourhood_attention(q, k, v, bias_tables, layer, batch, seq):
    n_blocks = seq // NA_QB
    q3, k3, v3 = (t.reshape(batch, seq, NA_WIDTH) for t in (q, k, v))
    blk = lambda shift: pl.BlockSpec(
        (None, NA_QB, NA_WIDTH), lambda b, i: (b, jnp.clip(i + shift, 0, n_blocks - 1), 0))
    variant = lambda b, i: (layer, jnp.where(i == 0, 0, jnp.where(i == n_blocks - 1, 2, 1)), 0, 0, 0)
    out = pl.pallas_call(
        _na_kernel,
        grid=(batch, n_blocks),
        in_specs=[blk(0), blk(-1), blk(0), blk(1), blk(-1), blk(0), blk(1),
                  pl.BlockSpec((None, None, NA_HEADS, NA_QB, 3 * NA_QB), variant)],
        out_specs=blk(0),
        out_shape=jax.ShapeDtypeStruct((batch, seq, NA_WIDTH), BF16),
        compiler_params=_cparams(2),
        name="natten",
    )(q3, k3, k3, k3, v3, v3, v3, bias_tables)
    return out.reshape(batch * seq, NA_WIDTH)


def _cmul(ar, ai, br, bi):
    return ar * br - ai * bi, ar * bi + ai * br


def _ssm_tables(a_re, a_im, log_dt, b_re, b_im, c_re, c_im):
    hi = lax.Precision.HIGHEST
    dt = jnp.exp(log_dt)[..., None]
    zr, zi = a_re * dt, a_im * dt
    taus = jnp.arange(SSM_CHUNK + 1, dtype=F32)[:, None, None, None]
    mag = jnp.exp(zr[None] * taus)
    pr, pi = mag * jnp.cos(zi[None] * taus), mag * jnp.sin(zi[None] * taus)
    lbr, lbi = pr[1], pi[1]
    den = a_re * a_re + a_im * a_im
    fr = ((lbr - 1.0) * a_re + lbi * a_im) / den
    fi = (lbi * a_re - (lbr - 1.0) * a_im) / den
    bbr, bbi = _cmul(fr[..., None], fi[..., None], b_re, b_im)

    wr, wi = _cmul(c_re[:, None], c_im[:, None],
                   pr[:SSM_CHUNK].transpose(1, 0, 2, 3)[:, :, :, None, :],
                   pi[:SSM_CHUNK].transpose(1, 0, 2, 3)[:, :, :, None, :])
    kern = (jnp.einsum('dtgop,dgpc->dtgoc', wr, bbr, precision=hi)
            - jnp.einsum('dtgop,dgpc->dtgoc', wi, bbi, precision=hi))
    kf, kb = kern[0], kern[1]
    kmix = jnp.concatenate([kb[:0:-1], (kf[0] + kb[0])[None], kf[1:]], axis=0)
    t_idx = np.arange(SSM_CHUNK)
    delta = t_idx[None, :] - t_idx[:, None] + SSM_CHUNK - 1
    intra = kmix[delta]
    intra = intra.transpose(2, 0, 4, 1, 3).reshape(SSM_GROUPS, SSM_CK, SSM_CK)

    pf_r, pf_i = _cmul(pr[SSM_CHUNK - 1::-1, 0][..., None], pi[SSM_CHUNK - 1::-1, 0][..., None],
                       bbr[0][None], bbi[0][None])
    pb_r, pb_i = _cmul(pr[:SSM_CHUNK, 1][..., None], pi[:SSM_CHUNK, 1][..., None], bbr[1][None], bbi[1][None])
    sp = jnp.stack([pf_r, pf_i, pb_r, pb_i], axis=0)
    sp = sp.transpose(2, 1, 4, 0, 3).reshape(SSM_GROUPS, SSM_CK, 4, SSM_STATE)
    sp = sp.reshape(SSM_PAIRS, 2, SSM_CK, 4, 1, SSM_STATE)
    eye = jnp.eye(2, dtype=F32).reshape(1, 2, 1, 1, 2, 1)
    sproj = (sp * eye).reshape(SSM_PAIRS, 2, SSM_CK, 4 * 2 * SSM_STATE)

    qf_r, qf_i = _cmul(c_re[0][None], c_im[0][None],
                       pr[1:, 0][:, :, None, :], pi[1:, 0][:, :, None, :])
    qb_r, qb_i = _cmul(c_re[1][None], c_im[1][None],
                       pr[:0:-1, 1][:, :, None, :], pi[:0:-1, 1][:, :, None, :])
    qp = jnp.stack([qf_r, -qf_i, qb_r, -qb_i], axis=0)
    qp = qp.transpose(2, 0, 4, 1, 3).reshape(SSM_PAIRS, 2, 4, SSM_STATE, 1, SSM_CK)
    eye = jnp.eye(2, dtype=F32).reshape(1, 2, 1, 1, 2, 1)
    qproj = (qp * eye).transpose(0, 2, 1, 3, 4, 5).reshape(SSM_PAIRS, 4, 2 * SSM_STATE, 2 * SSM_CK)

    steps = (jnp.arange(1, 9, dtype=F32) * SSM_CHUNK)[:, None, None, None]
    mag = jnp.exp(zr[None] * steps)
    sr, si = mag * jnp.cos(zi[None] * steps), mag * jnp.sin(zi[None] * steps)
    n_state = SSM_GROUPS * SSM_STATE
    lam_pows = jnp.stack([jnp.stack([sr[:, 0], si[:, 0]]), jnp.stack([sr[::-1, 1], si[::-1, 1]])])
    lam_pows = lam_pows.reshape(4 * 8, n_state)
    return intra.astype(BF16), sproj.astype(BF16), qproj.astype(BF16), lam_pows


SSM_ROW_TILE = 1024
SSM_SCAN_LANES = 128


def _ssm_state_kernel(u_ref, sp_ref, st_ref):
    st = (jnp.dot(u_ref[0], sp_ref[0], preferred_element_type=F32)
          + jnp.dot(u_ref[1], sp_ref[1], preferred_element_type=F32))
    for part in range(4):
        st_ref[part] = st[:, 128 * part:128 * (part + 1)]


def _ssm_scan_kernel(st_ref, lam_ref, xin_ref):
    n_seq, n_tiles, lanes = st_ref.shape[1], st_ref.shape[2] // 8, st_ref.shape[3]
    row = lax.broadcasted_iota(jnp.int32, (8, lanes), 0)

    def bcast(x, r):
        return jnp.broadcast_to(x[r:r + 1, :], (8, lanes))

    tabs = []
    for d in range(2):
        pr, pi = lam_ref[16 * d:16 * d + 8, :], lam_ref[16 * d + 8:16 * d + 16, :]
        at = (lambda m: m - 1) if d == 0 else (lambda m: 8 - m)
        tabs.append((pr, pi, [(bcast(pr, at(m)), bcast(pi, at(m))) for m in (1, 2, 4)]))

    def shifted(x, dist, forward):
        if forward:
            return jnp.where(row >= dist, pltpu.roll(x, dist, 0), 0.0)
        return jnp.where(row < 8 - dist, pltpu.roll(x, 8 - dist, 0), 0.0)

    def scan_tile(sr, si, cr, ci, d):
        forward = d == 0
        pr, pi, steps = tabs[d]
        hr, hi = sr, si
        for (lr, li), dist in zip(steps, (1, 2, 4)):
            mr, mi = _cmul(lr, li, shifted(hr, dist, forward), shifted(hi, dist, forward))
            hr, hi = hr + mr, hi + mi
        mr, mi = _cmul(pr, pi, cr, ci)
        er, ei = hr + mr, hi + mi
        if forward:
            xr = jnp.where(row >= 1, pltpu.roll(er, 1, 0), cr)
            xi = jnp.where(row >= 1, pltpu.roll(ei, 1, 0), ci)
            return xr, xi, bcast(er, 7), bcast(ei, 7)
        xr = jnp.where(row < 7, pltpu.roll(er, 7, 0), cr)
        xi = jnp.where(row < 7, pltpu.roll(ei, 7, 0), ci)
        return xr, xi, bcast(er, 0), bcast(ei, 0)

    def body(i, carry):
        new = []
        for s in range(n_seq):
            for d in range(2):
                cr, ci = carry[2 * (2 * s + d)], carry[2 * (2 * s + d) + 1]
                r0 = pl.multiple_of((i if d == 0 else n_tiles - 1 - i) * 8, 8)
                sr, si = st_ref[2 * d, s, pl.ds(r0, 8), :], st_ref[2 * d + 1, s, pl.ds(r0, 8), :]
                xr, xi, ncr, nci = scan_tile(sr, si, cr, ci, d)
                xin_ref[2 * d, s, pl.ds(r0, 8), :] = xr
                xin_ref[2 * d + 1, s, pl.ds(r0, 8), :] = xi
                new += [ncr, nci]
        return tuple(new)

    zero = jnp.zeros((8, lanes), F32)
    lax.fori_loop(0, n_tiles, body, (zero,) * (4 * n_seq))


def _ssm_out_kernel(u_ref, xin_ref, mi_ref, qp_ref, o_ref):
    acc = jnp.dot(xin_ref[0].astype(BF16), qp_ref[0], preferred_element_type=F32)
    for part in range(1, 4):
        acc = acc + jnp.dot(xin_ref[part].astype(BF16), qp_ref[part], preferred_element_type=F32)
    for e in range(2):
        y0 = jnp.dot(u_ref[e], mi_ref[e], preferred_element_type=F32)
        o_ref[e] = y0 + acc[:, SSM_CK * e:SSM_CK * (e + 1)]


def _ssm_scan(upack, batch, seq, tables, layer):
    intra, sproj, qproj, lam_pows = tables
    n_chunks = seq // SSM_CHUNK
    rows = n_chunks * batch
    n_state = SSM_GROUPS * SSM_STATE
    tr = SSM_ROW_TILE
    st = pl.pallas_call(
        _ssm_state_kernel,
        grid=(SSM_PAIRS, rows // tr),
        in_specs=[pl.BlockSpec((2, tr, SSM_CK), lambda p, r: (p, r, 0)),
                  pl.BlockSpec((None, None, 2, SSM_CK, 512), lambda p, r: (layer, p, 0, 0, 0))],
        out_specs=pl.BlockSpec((4, tr, 128), lambda p, r: (0, r, p)),
        out_shape=jax.ShapeDtypeStruct((4, rows, n_state), F32),
        compiler_params=_cparams(2),
        name="ssm_state",
    )(upack, sproj)
    lanes = SSM_SCAN_LANES
    seq_block = pl.BlockSpec((4, batch, n_chunks, lanes), lambda j: (0, 0, 0, j))
    xin = pl.pallas_call(
        _ssm_scan_kernel,
        grid=(n_state // lanes,),
        in_specs=[seq_block, pl.BlockSpec((None, 32, lanes), lambda j: (layer, 0, j))],
        out_specs=seq_block,
        out_shape=jax.ShapeDtypeStruct((4, batch, n_chunks, n_state), F32),
        compiler_params=_cparams(1),
        name="ssm_scan",
    )(st.reshape(4, batch, n_chunks, n_state), lam_pows)
    return pl.pallas_call(
        _ssm_out_kernel,
        grid=(SSM_PAIRS, rows // tr),
        in_specs=[pl.BlockSpec((2, tr, SSM_CK), lambda p, r: (p, r, 0)),
                  pl.BlockSpec((4, tr, 128), lambda p, r: (0, r, p)),
                  pl.BlockSpec((None, 2, SSM_CK, SSM_CK), lambda p, r: (layer, p, 0, 0)),
                  pl.BlockSpec((None, None, 4, 128, 2 * SSM_CK), lambda p, r: (layer, p, 0, 0, 0))],
        out_specs=pl.BlockSpec((2, tr, SSM_CK), lambda p, r: (p, r, 0)),
        out_shape=jax.ShapeDtypeStruct((SSM_GROUPS, rows, SSM_CK), F32),
        compiler_params=_cparams(2),
        name="ssm_out",
    )(upack, xin.reshape(4, rows, n_state), intra, qproj)


def _merge_kernel(x_ref, gate_ref, fm_ref, na_ref, ypack_ref, ussm_ref, dskip_ref, wglu_ref,
                  wfn_ref, wna_ref, wssm_ref, wout_ref, o_ref, ytok_ref, yrow_ref):
    y_fn = jnp.dot(fm_ref[...], wfn_ref[...], preferred_element_type=F32)
    y_na = jnp.dot(na_ref[...], wna_ref[...], preferred_element_type=F32)
    merged = (gate_ref[:, :D_MODEL].astype(F32) * y_fn
              + gate_ref[:, D_MODEL:2 * D_MODEL].astype(F32) * y_na)
    for t in range(SSM_CHUNK):
        for g in range(SSM_GROUPS):
            yrow_ref[:, g * SSM_GROUP_DIM:(g + 1) * SSM_GROUP_DIM] = (
                ypack_ref[g, :, t * SSM_GROUP_DIM:(t + 1) * SSM_GROUP_DIM])
        ytok_ref[:, t, :] = yrow_ref[...]
    yscan = ytok_ref[...].reshape(TOKEN_TILE, SSM_WIDTH)
    y = yscan + dskip_ref[...] * ussm_ref[...]
    y = jax.nn.gelu(y)
    y = y * jax.nn.sigmoid(jnp.dot(y.astype(BF16), wglu_ref[...], preferred_element_type=F32))
    y_ssm = jnp.dot(y.astype(BF16), wssm_ref[...], preferred_element_type=F32)
    merged = merged + gate_ref[:, 2 * D_MODEL:].astype(F32) * y_ssm
    o_ref[...] = x_ref[...] + jnp.dot(merged.astype(BF16), wout_ref[...], preferred_element_type=F32)


def _merge(x, gates, fm, na, ypack, ussm, ssm_d, w_glu, w_br_fn, w_br_na, w_br_ssm, w_out, layer):
    n = x.shape[0]
    tm = TOKEN_TILE
    row = lambda width: pl.BlockSpec((tm, width), lambda i: (i, 0))
    wspec = lambda k, m: _resident((None, k, m), lambda i: (layer, 0, 0))
    return pl.pallas_call(
        _merge_kernel,
        grid=(n // tm,),
        in_specs=[row(D_MODEL), row(N_BRANCH * D_MODEL), row(FN_WIDTH), row(NA_WIDTH),
                  pl.BlockSpec((SSM_GROUPS, tm // SSM_CHUNK, SSM_CK), lambda i: (0, i, 0)),
                  row(SSM_WIDTH), wspec(1, SSM_WIDTH), wspec(SSM_WIDTH, SSM_WIDTH),
                  wspec(FN_WIDTH, D_MODEL), wspec(NA_WIDTH, D_MODEL), wspec(SSM_WIDTH, D_MODEL),
                  wspec(D_MODEL, D_MODEL)],
        out_specs=row(D_MODEL),
        out_shape=jax.ShapeDtypeStruct((n, D_MODEL), F32),
        scratch_shapes=[pltpu.VMEM((tm // SSM_CHUNK, SSM_CHUNK, SSM_WIDTH), F32),
                        pltpu.VMEM((tm // SSM_CHUNK, SSM_WIDTH), F32)],
        compiler_params=_cparams(1),
        name="merge",
    )(x, gates, fm, na, ypack, ussm, ssm_d, w_glu, w_br_fn, w_br_na, w_br_ssm, w_out)


def _ffn_kernel(x_ref, g_ref, wup_ref, wdown_ref, gfin_ref, o_ref, up_ref, *, final_norm):
    x = x_ref[...]
    h = _rms(x, g_ref[...]).astype(BF16)
    for c in range(D_FF // D_MODEL):
        cols = slice(c * D_MODEL, (c + 1) * D_MODEL)
        a = jnp.maximum(jnp.dot(h, wup_ref[:, cols], preferred_element_type=F32), 0.0)
        up_ref[:, cols] = (a * a).astype(BF16)
    y = x + jnp.dot(up_ref[...], wdown_ref[...], preferred_element_type=F32)
    if final_norm:
        y = _rms(y, gfin_ref[...])
    o_ref[...] = y


def _ffn(x, g_ffn, w_up, w_down, g_final, layer, final_norm):
    n = x.shape[0]
    tm = TOKEN_TILE
    row = pl.BlockSpec((tm, D_MODEL), lambda i: (i, 0))
    return pl.pallas_call(
        functools.partial(_ffn_kernel, final_norm=final_norm),
        grid=(n // tm,),
        in_specs=[row,
                  _resident((None, 1, D_MODEL), lambda i: (layer, 0, 0)),
                  _resident((None, D_MODEL, D_FF), lambda i: (layer, 0, 0)),
                  _resident((None, D_FF, D_MODEL), lambda i: (layer, 0, 0)),
                  _resident((1, D_MODEL), lambda i: (0, 0))],
        out_specs=row,
        out_shape=jax.ShapeDtypeStruct((n, D_MODEL), F32),
        scratch_shapes=[pltpu.VMEM((tm, D_FF), BF16)],
        compiler_params=_cparams(1),
        name="ffn",
    )(x, g_ffn, w_up, w_down, g_final)


def kernel(x, g_mix, w_in, na_rpb, ssm_a_re, ssm_a_im, ssm_log_dt, ssm_b_re, ssm_b_im, ssm_c_re, ssm_c_im,
           ssm_d, w_glu, w_br_fn, w_br_na, w_br_ssm, w_out, g_ffn, w_up, w_down, g_final):
    batch, seq, _ = x.shape
    depth = w_in.shape[0]
    assert seq == DFT_N1 * DFT_N2 and seq % (GRID_W * NA_RQ) == 0 and (batch * seq) % TOKEN_TILE == 0
    rows = seq // GRID_W

    to_bf16 = lambda w: w.astype(BF16)
    w_in, w_glu, w_br_fn, w_br_na, w_br_ssm, w_out, w_up, w_down = map(
        to_bf16, (w_in, w_glu, w_br_fn, w_br_na, w_br_ssm, w_out, w_up, w_down))
    g_mix3 = g_mix.reshape(depth, 1, D_MODEL)
    g_ffn3 = g_ffn.reshape(depth, 1, D_MODEL)
    ssm_d3 = ssm_d.reshape(depth, 1, SSM_WIDTH)
    g_final2 = g_final.reshape(1, D_MODEL)
    dft_tables = _dft_tables(seq)
    na_tables = _na_bias_tables(na_rpb, rows)
    ssm_tables = jax.vmap(_ssm_tables)(ssm_a_re, ssm_a_im, ssm_log_dt, ssm_b_re, ssm_b_im, ssm_c_re, ssm_c_im)

    xs = x.reshape(batch * seq, D_MODEL)
    for l in range(depth):
        ufn, q, k, v, ussm, upack, gates = _inproj(xs, g_mix3, w_in, l)
        fm = _fourier_mix(ufn, batch, seq, dft_tables)
        na = _neighbourhood_attention(q, k, v, na_tables, l, batch, seq)
        ypack = _ssm_scan(upack, batch, seq, ssm_tables, l)
        xs = _merge(xs, gates, fm, na, ypack, ussm, ssm_d3, w_glu, w_br_fn, w_br_na, w_br_ssm, w_out, l)
        xs = _ffn(xs, g_ffn3, w_up, w_down, g_final2, l, final_norm=(l == depth - 1))
    return xs.reshape(batch, seq, D_MODEL)
```

```python
import functools
import math

import numpy as np
import jax
import jax.numpy as jnp
from jax import lax
from jax.experimental import pallas as pl
from jax.experimental.pallas import tpu as pltpu

F32 = jnp.float32
BF16 = jnp.bfloat16

D_MODEL = 1024
GRID_W = 64
FN_GROUP_DIM = 64
FN_WIDTH = 256
NA_HEADS = 8
NA_HEAD_DIM = 64
NA_WIDTH = 512
NA_KR = 8
NA_KC = 16
NEG_INF = -1e30
SSM_GROUPS = 16
SSM_GROUP_DIM = 16
SSM_WIDTH = 256
SSM_STATE = 64
N_BRANCH = 3
D_FF = 4 * D_MODEL
D_IN = FN_WIDTH + 3 * NA_WIDTH + SSM_WIDTH + N_BRANCH * D_MODEL
RMS_EPS = 1e-6
LOG2_E = 1.0 / math.log(2.0)

DFT_N1 = 128
DFT_N2 = 64
NA_RQ = 4
NA_QB = NA_RQ * GRID_W
SSM_CHUNK = 16
SSM_PAIRS = SSM_GROUPS // 2
SSM_CK = SSM_CHUNK * SSM_GROUP_DIM

TOKEN_TILE = 512
VMEM_LIMIT = 56 * 1024 * 1024


def _cparams(n_axes):
    return pltpu.CompilerParams(dimension_semantics=("arbitrary",) * n_axes, vmem_limit_bytes=VMEM_LIMIT)


def _resident(block_shape, index_map):
    return pl.BlockSpec(block_shape, index_map, pipeline_mode=pl.Buffered(1))


def _rms(x, g):
    ms = jnp.mean(x * x, axis=-1, keepdims=True)
    return x * lax.rsqrt(ms + RMS_EPS) * g


def _inproj_kernel(x_ref, g_ref, w_ref, ufn_ref, q_ref, k_ref, v_ref, ussm_ref, upack_ref, gate_ref):
    h = _rms(x_ref[...], g_ref[...]).astype(BF16)

    def seg(lo, width):
        return jnp.dot(h, w_ref[:, lo:lo + width], preferred_element_type=F32)

    ufn_ref[...] = seg(0, FN_WIDTH)
    q_ref[...] = (seg(FN_WIDTH, NA_WIDTH) * (NA_HEAD_DIM ** -0.5 * LOG2_E)).astype(BF16)
    k_ref[...] = seg(FN_WIDTH + NA_WIDTH, NA_WIDTH).astype(BF16)
    v_ref[...] = seg(FN_WIDTH + 2 * NA_WIDTH, NA_WIDTH).astype(BF16)
    ussm = seg(FN_WIDTH + 3 * NA_WIDTH, SSM_WIDTH)
    ussm_ref[...] = ussm
    u3 = ussm.astype(BF16).reshape(TOKEN_TILE // SSM_CHUNK, SSM_CHUNK, SSM_WIDTH)
    for t in range(SSM_CHUNK):
        ut = u3[:, t, :]
        for g in range(SSM_GROUPS):
            upack_ref[g, :, t * SSM_GROUP_DIM:(t + 1) * SSM_GROUP_DIM] = (
                ut[:, g * SSM_GROUP_DIM:(g + 1) * SSM_GROUP_DIM])
    gate_lo = FN_WIDTH + 3 * NA_WIDTH + SSM_WIDTH
    for j in range(N_BRANCH):
        gate_ref[:, j * D_MODEL:(j + 1) * D_MODEL] = jax.nn.sigmoid(seg(gate_lo + j * D_MODEL, D_MODEL)).astype(BF16)


def _inproj(x, g_mix, w_in, layer):
    n = x.shape[0]
    tm = TOKEN_TILE
    row = lambda width: pl.BlockSpec((tm, width), lambda i: (i, 0))
    return pl.pallas_call(
        _inproj_kernel,
        grid=(n // tm,),
        in_specs=[row(D_MODEL),
                  _resident((None, 1, D_MODEL), lambda i: (layer, 0, 0)),
                  _resident((None, D_MODEL, D_IN), lambda i: (layer, 0, 0))],
        out_specs=[row(FN_WIDTH), row(NA_WIDTH), row(NA_WIDTH), row(NA_WIDTH), row(SSM_WIDTH),
                   pl.BlockSpec((SSM_GROUPS, tm // SSM_CHUNK, SSM_CK), lambda i: (0, i, 0)),
                   row(N_BRANCH * D_MODEL)],
        out_shape=[jax.ShapeDtypeStruct((n, FN_WIDTH), F32),
                   jax.ShapeDtypeStruct((n, NA_WIDTH), BF16),
                   jax.ShapeDtypeStruct((n, NA_WIDTH), BF16),
                   jax.ShapeDtypeStruct((n, NA_WIDTH), BF16),
                   jax.ShapeDtypeStruct((n, SSM_WIDTH), F32),
                   jax.ShapeDtypeStruct((SSM_GROUPS, n // SSM_CHUNK, SSM_CK), BF16),
                   jax.ShapeDtypeStruct((n, N_BRANCH * D_MODEL), BF16)],
        compiler_params=_cparams(1),
        name="inproj",
    )(x, g_mix, w_in)


def _dft_tables(seq):
    two_pi = 2.0 * math.pi
    k1 = lax.broadcasted_iota(jnp.int32, (DFT_N1, DFT_N1), 0)
    n1 = lax.broadcasted_iota(jnp.int32, (DFT_N1, DFT_N1), 1)
    ang = ((k1 * n1) % DFT_N1).astype(F32) * (two_pi / DFT_N1)
    wa = jnp.concatenate([jnp.cos(ang), -jnp.sin(ang)], axis=0)

    shape = (DFT_N1, DFT_N2, DFT_N2)
    kk1 = lax.broadcasted_iota(jnp.int32, shape, 0)
    kk2 = lax.broadcasted_iota(jnp.int32, shape, 1)
    nn2 = lax.broadcasted_iota(jnp.int32, shape, 2)
    ang = ((nn2 * (kk1 + DFT_N1 * kk2)) % seq).astype(F32) * (two_pi / seq)
    c, s = jnp.cos(ang), jnp.sin(ang)
    g = jnp.concatenate([jnp.concatenate([c, s], axis=2), jnp.concatenate([-s, c], axis=2)], axis=1)

    m = lax.broadcasted_iota(jnp.int32, (FN_WIDTH, FN_WIDTH), 0)
    j = lax.broadcasted_iota(jnp.int32, (FN_WIDTH, FN_WIDTH), 1)
    same_group = (m // FN_GROUP_DIM) == (j // FN_GROUP_DIM)
    ang = ((m * j) % FN_GROUP_DIM).astype(F32) * (two_pi / FN_GROUP_DIM)
    scale = 1.0 / math.sqrt(seq * FN_GROUP_DIM)
    cc = jnp.where(same_group, jnp.cos(ang) * scale, 0.0)
    sc = jnp.where(same_group, jnp.sin(ang) * scale, 0.0)
    return wa.astype(BF16), g.astype(BF16), cc.astype(BF16), sc.astype(BF16)


DFT_A_N2 = 8
DFT_C_K1 = 16


def _dft_a_kernel(w_ref, x_ref, o_ref):
    for j in range(DFT_A_N2):
        a = jnp.dot(w_ref[...], x_ref[:, j, :].astype(BF16), preferred_element_type=F32)
        o_ref[0, :, j, :] = a[:DFT_N1]
        o_ref[1, :, j, :] = a[DFT_N1:]


def _dft_c_kernel(ar_ref, ai_ref, g_ref, cc_ref, sc_ref, o_ref, zr_ref, zi_ref):
    for j in range(DFT_C_K1):
        a = jnp.concatenate([ar_ref[j], ai_ref[j]], axis=0).astype(BF16)
        z = jnp.dot(g_ref[j], a, preferred_element_type=F32)
        zr_ref[j * DFT_N2:(j + 1) * DFT_N2, :] = z[:DFT_N2].astype(BF16)
        zi_ref[j * DFT_N2:(j + 1) * DFT_N2, :] = z[DFT_N2:].astype(BF16)
    o = (jnp.dot(zr_ref[...], cc_ref[...], preferred_element_type=F32)
         + jnp.dot(zi_ref[...], sc_ref[...], preferred_element_type=F32)).astype(BF16)
    for j in range(DFT_C_K1):
        o_ref[:, j, :] = o[j * DFT_N2:(j + 1) * DFT_N2]


def _fourier_mix(ufn, batch, seq, tables):
    wa, g, cc, sc = tables
    x = ufn.reshape(batch, DFT_N1, DFT_N2, FN_WIDTH)
    nb = DFT_A_N2
    a = pl.pallas_call(
        _dft_a_kernel,
        grid=(batch, DFT_N2 // nb),
        in_specs=[_resident((2 * DFT_N1, DFT_N1), lambda b, j: (0, 0)),
                  pl.BlockSpec((None, DFT_N1, nb, FN_WIDTH), lambda b, j: (b, 0, j, 0))],
        out_specs=pl.BlockSpec((None, 2, DFT_N1, nb, FN_WIDTH), lambda b, j: (b, 0, 0, j, 0)),
        out_shape=jax.ShapeDtypeStruct((batch, 2, DFT_N1, DFT_N2, FN_WIDTH), F32),
        compiler_params=_cparams(2),
        name="dft_a",
    )(wa, x)
    kb = DFT_C_K1
    part = lambda p: pl.BlockSpec((None, None, kb, DFT_N2, FN_WIDTH), lambda b, i: (b, p, i, 0, 0))
    out = pl.pallas_call(
        _dft_c_kernel,
        grid=(batch, DFT_N1 // kb),
        in_specs=[part(0), part(1),
                  pl.BlockSpec((kb, 2 * DFT_N2, 2 * DFT_N2), lambda b, i: (i, 0, 0)),
                  _resident((FN_WIDTH, FN_WIDTH), lambda b, i: (0, 0)),
                  _resident((FN_WIDTH, FN_WIDTH), lambda b, i: (0, 0))],
        out_specs=pl.BlockSpec((None, DFT_N2, kb, FN_WIDTH), lambda b, i: (b, 0, i, 0)),
        out_shape=jax.ShapeDtypeStruct((batch, DFT_N2, DFT_N1, FN_WIDTH), BF16),
        scratch_shapes=[pltpu.VMEM((kb * DFT_N2, FN_WIDTH), BF16), pltpu.VMEM((kb * DFT_N2, FN_WIDTH), BF16)],
        compiler_params=_cparams(2),
        name="dft_c",
    )(a, a, g, cc, sc)
    return out.reshape(batch * seq, FN_WIDTH)


NA_DR = 2 * NA_KR - 1


def _na_bias_slabs(rpb):
    depth = rpb.shape[0]
    pad = GRID_W - NA_KC
    period = 2 * GRID_W - 1
    ext = jnp.pad(rpb * LOG2_E, ((0, 0), (0, 0), (0, 0), (pad, pad)))
    flat = jnp.tile(ext, (1, 1, 1, GRID_W))[..., GRID_W - 1:GRID_W - 1 + GRID_W * (period - 1)]
    slab = flat.reshape(depth, NA_HEADS, NA_DR, GRID_W, period - 1)[..., :GRID_W]
    qc, kc = np.arange(GRID_W)[:, None], np.arange(GRID_W)[None, :]
    col_start = np.clip(qc - NA_KC // 2, 0, GRID_W - NA_KC)
    col_ok = (kc >= col_start) & (kc < col_start + NA_KC)
    slab = jnp.where(col_ok, slab, NEG_INF)
    return jnp.concatenate([slab[:, :, :-1], slab[:, :, 1:]], axis=-1)


def _na_row_masks(rows):
    kr = min(NA_KR, rows)
    n_blocks = rows // NA_RQ
    out = np.zeros((3, 2, 3, NA_QB, 2 * NA_HEAD_DIM), np.float32)
    for v, blk in enumerate((0, 1, n_blocks - 1)):
        r = blk * NA_RQ + np.arange(NA_RQ)[:, None]
        key_row = (blk - 1) * NA_RQ + np.arange(3 * NA_RQ)[None, :]
        row_start = np.clip(r - kr // 2, 0, rows - kr)
        row_ok = (key_row >= row_start) & (key_row < row_start + kr)
        pen = np.where(row_ok, 0.0, NEG_INF).reshape(NA_RQ, 3, NA_RQ)
        pen = np.repeat(pen.transpose(1, 2, 0), GRID_W, axis=1)
        for e in range(2):
            lo = NA_HEAD_DIM * (1 - e)
            out[v, e, :, :, lo:lo + NA_RQ] = pen
    return out


def _na_kernel(q_ref, kp_ref, kc_ref, kn_ref, vp_ref, vc_ref, vn_ref, slab_ref, rmask_ref, o_ref):
    lane = lax.broadcasted_iota(jnp.int32, (1, 2 * NA_HEAD_DIM), 1)
    first = lane < NA_HEAD_DIM
    q_row = lax.broadcasted_iota(jnp.int32, (NA_QB, 2 * NA_HEAD_DIM), 0) // GRID_W
    q_lane = lax.broadcasted_iota(jnp.int32, (NA_QB, 2 * NA_HEAD_DIM), 1)
    k_refs = (kp_ref, kc_ref, kn_ref)
    v_refs = (vp_ref, vc_ref, vn_ref)
    for hp in range(NA_HEADS // 2):
        sl = slice(2 * NA_HEAD_DIM * hp, 2 * NA_HEAD_DIM * (hp + 1))
        q2 = q_ref[:, sl]
        acc = []
        for e in range(2):
            mine = first if e == 0 else jnp.logical_not(first)
            row_onehot = (q_lane == NA_HEAD_DIM * (1 - e) + q_row).astype(BF16)
            qm = jnp.where(mine, q2, row_onehot)
            ms, rs = [], []
            for i in range(3):
                km = jnp.where(mine, k_refs[i][:, sl], rmask_ref[e, i])
                s = lax.dot_general(qm, km, (((1,), (1,)), ((), ())), preferred_element_type=F32)
                bias = jnp.concatenate(
                    [jnp.concatenate([slab_ref[2 * hp + e, NA_RQ * i + 2 * m - a + NA_KR - 1 - NA_RQ]
                                      for m in range(NA_RQ // 2)], axis=1) for a in range(NA_RQ)], axis=0)
                s = s + bias
                m = jnp.max(s, axis=-1, keepdims=True)
                p = jnp.exp2(s - m).astype(BF16)
                v2 = v_refs[i][:, sl]
                vm = jnp.where(mine, v2, jnp.ones_like(v2))
                ms.append(m)
                rs.append(jnp.dot(p, vm, preferred_element_type=F32))
            m_all = jnp.maximum(jnp.maximum(ms[0], ms[1]), ms[2])
            acc.append(rs[0] * jnp.exp2(ms[0] - m_all) + rs[1] * jnp.exp2(ms[1] - m_all)
                       + rs[2] * jnp.exp2(ms[2] - m_all))
        num = jnp.where(first, acc[0], acc[1])
        den = pltpu.roll(jnp.where(first, acc[1], acc[0]), NA_HEAD_DIM, 1)
        o_ref[:, sl] = (num / den).astype(BF16)


def _neighbourhood_attention(q, k, v, slabs, row_masks, layer, batch, seq):
    n_blocks = seq // NA_QB
    q3, k3, v3 = (t.reshape(batch, seq, NA_WIDTH) for t in (q, k, v))
    blk = lambda shift: pl.BlockSpec(
        (None, NA_QB, NA_WIDTH), lambda b, i: (b, jnp.clip(i + shift, 0, n_blocks - 1), 0))
    variant = lambda b, i: (jnp.where(i == 0, 0, jnp.where(i == n_blocks - 1, 2, 1)), 0, 0, 0, 0)
    out = pl.pallas_call(
        _na_kernel,
        grid=(batch, n_blocks),
        in_specs=[blk(0), blk(-1), blk(0), blk(1), blk(-1), blk(0), blk(1),
                  _resident((None, NA_HEADS, NA_DR - 1, GRID_W, 2 * GRID_W), lambda b, i: (layer, 0, 0, 0, 0)),
                  pl.BlockSpec((None, 2, 3, NA_QB, 2 * NA_HEAD_DIM), variant)],
        out_specs=blk(0),
        out_shape=jax.ShapeDtypeStruct((batch, seq, NA_WIDTH), BF16),
        compiler_params=_cparams(2),
        name="natten",
    )(q3, k3, k3, k3, v3, v3, v3, slabs, row_masks)
    return out.reshape(batch * seq, NA_WIDTH)


def _cmul(ar, ai, br, bi):
    return ar * br - ai * bi, ar * bi + ai * br


def _ssm_table_kernel(pr_ref, pi_ref, bbr_ref, bbi_ref, cr_ref, ci_ref, intra_ref, sp_ref, qt_ref):
    n, c = SSM_CHUNK, SSM_GROUP_DIM
    hi = lax.Precision.HIGHEST
    lanes_contract = (((1,), (1,)), ((), ()))
    sp_ref[...] = jnp.zeros(sp_ref.shape, sp_ref.dtype)
    qt_ref[...] = jnp.zeros(qt_ref.shape, qt_ref.dtype)
    spread = (lax.broadcasted_iota(jnp.int32, (c, SSM_CK), 1) % c
              == lax.broadcasted_iota(jnp.int32, (c, SSM_CK), 0)).astype(F32)
    delta = (lax.broadcasted_iota(jnp.int32, (SSM_CK, SSM_CK), 1) // c
             - lax.broadcasted_iota(jnp.int32, (SSM_CK, SSM_CK), 0) // c)
    for e in range(2):
        toeplitz = []
        for d in range(2):
            pr, pi = pr_ref[d, e], pi_ref[d, e]
            w = [_cmul(bbr_ref[d, e], bbi_ref[d, e], pr[t:t + 1], pi[t:t + 1]) for t in range(n)]
            wr = jnp.concatenate([x[0] for x in w], axis=0)
            wi = jnp.concatenate([x[1] for x in w], axis=0)
            kt = (lax.dot_general(wr, cr_ref[d, e], lanes_contract, precision=hi, preferred_element_type=F32)
                  - lax.dot_general(wi, ci_ref[d, e], lanes_contract, precision=hi, preferred_element_type=F32))
            toeplitz.append(jnp.dot(kt, spread, precision=hi, preferred_element_type=F32))
            order = range(n - 1, -1, -1) if d == 0 else range(n)
            for ri in range(2):
                col = (2 * d + ri) * 2 * SSM_STATE + e * SSM_STATE
                sp_ref[e, :, col:col + SSM_STATE] = jnp.concatenate([w[t][ri] for t in order], axis=0).astype(BF16)
            exps = range(1, n + 1) if d == 0 else range(n, 0, -1)
            q = [_cmul(cr_ref[d, e], ci_ref[d, e], pr[m:m + 1], pi[m:m + 1]) for m in exps]
            rows = slice(e * SSM_CK, (e + 1) * SSM_CK)
            cols = slice(e * SSM_STATE, (e + 1) * SSM_STATE)
            qt_ref[2 * d, rows, cols] = jnp.concatenate([x[0] for x in q], axis=0).astype(BF16)
            qt_ref[2 * d + 1, rows, cols] = (-jnp.concatenate([x[1] for x in q], axis=0)).astype(BF16)
        fwd, bwd = toeplitz
        acc = jnp.zeros((SSM_CK, SSM_CK), F32)
        for dlt in range(-(n - 1), n):
            if dlt > 0:
                src = fwd[dlt * c:(dlt + 1) * c]
            elif dlt < 0:
                src = bwd[-dlt * c:(-dlt + 1) * c]
            else:
                src = fwd[:c] + bwd[:c]
            acc = jnp.where(delta == dlt, jnp.concatenate([src] * n, axis=0), acc)
        intra_ref[e] = acc.astype(BF16)


def _ssm_tables(a_re, a_im, log_dt, b_re, b_im, c_re, c_im):
    depth = a_re.shape[0]
    dt = jnp.exp(log_dt)[..., None]
    zr, zi = a_re * dt, a_im * dt

    def powers(taus):
        taus = taus.astype(F32)[:, None]
        mag = jnp.exp(zr[..., None, :] * taus)
        return mag * jnp.cos(zi[..., None, :] * taus), mag * jnp.sin(zi[..., None, :] * taus)

    pr, pi = powers(jnp.arange(SSM_CHUNK + 1))
    lbr, lbi = pr[..., 1, :], pi[..., 1, :]
    den = a_re * a_re + a_im * a_im
    fr = ((lbr - 1.0) * a_re + lbi * a_im) / den
    fi = (lbi * a_re - (lbr - 1.0) * a_im) / den
    bbr, bbi = _cmul(fr[..., None, :], fi[..., None, :],
                     jnp.swapaxes(b_re, -1, -2), jnp.swapaxes(b_im, -1, -2))

    pair = lambda rows: pl.BlockSpec((None, 2, 2, rows, SSM_STATE), lambda l, p: (l, 0, p, 0, 0))
    intra, sproj, qproj_t = pl.pallas_call(
        _ssm_table_kernel,
        grid=(depth, SSM_PAIRS),
        in_specs=[pair(SSM_CHUNK + 1), pair(SSM_CHUNK + 1), pair(SSM_GROUP_DIM), pair(SSM_GROUP_DIM),
                  pair(SSM_GROUP_DIM), pair(SSM_GROUP_DIM)],
        out_specs=[pl.BlockSpec((None, 2, SSM_CK, SSM_CK), lambda l, p: (l, p, 0, 0)),
                   pl.BlockSpec((None, None, 2, SSM_CK, 8 * SSM_STATE), lambda l, p: (l, p, 0, 0, 0)),
                   pl.BlockSpec((None, None, 4, 2 * SSM_CK, 2 * SSM_STATE), lambda l, p: (l, p, 0, 0, 0))],
        out_shape=[jax.ShapeDtypeStruct((depth, SSM_GROUPS, SSM_CK, SSM_CK), BF16),
                   jax.ShapeDtypeStruct((depth, SSM_PAIRS, 2, SSM_CK, 8 * SSM_STATE), BF16),
                   jax.ShapeDtypeStruct((depth, SSM_PAIRS, 4, 2 * SSM_CK, 2 * SSM_STATE), BF16)],
        compiler_params=_cparams(2),
        name="ssm_tables",
    )(pr, pi, bbr, bbi, c_re, c_im)

    sr, si = powers(jnp.arange(1, 9) * SSM_CHUNK)
    sr, si = sr.transpose(0, 1, 3, 2, 4), si.transpose(0, 1, 3, 2, 4)
    lam_pows = jnp.stack([jnp.stack([sr[:, 0], si[:, 0]], axis=1),
                          jnp.stack([sr[:, 1, ::-1], si[:, 1, ::-1]], axis=1)], axis=1)
    lam_pows = lam_pows.reshape(depth, 4 * 8, SSM_GROUPS * SSM_STATE)
    return intra, sproj, qproj_t, lam_pows


SSM_ROW_TILE = 1024
SSM_SCAN_LANES = 128


def _ssm_state_kernel(u_ref, sp_ref, st_ref):
    st = (jnp.dot(u_ref[0], sp_ref[0], preferred_element_type=F32)
          + jnp.dot(u_ref[1], sp_ref[1], preferred_element_type=F32))
    for part in range(4):
        st_ref[part] = st[:, 128 * part:128 * (part + 1)]


def _ssm_scan_kernel(st_ref, lam_ref, xin_ref):
    n_seq, n_tiles, lanes = st_ref.shape[1], st_ref.shape[2] // 8, st_ref.shape[3]
    row = lax.broadcasted_iota(jnp.int32, (8, lanes), 0)

    def bcast(x, r):
        return jnp.broadcast_to(x[r:r + 1, :], (8, lanes))

    tabs = []
    for d in range(2):
        pr, pi = lam_ref[16 * d:16 * d + 8, :], lam_ref[16 * d + 8:16 * d + 16, :]
        at = (lambda m: m - 1) if d == 0 else (lambda m: 8 - m)
        tabs.append((pr, pi, [(bcast(pr, at(m)), bcast(pi, at(m))) for m in (1, 2, 4)]))

    def shifted(x, dist, forward):
        if forward:
            return jnp.where(row >= dist, pltpu.roll(x, dist, 0), 0.0)
        return jnp.where(row < 8 - dist, pltpu.roll(x, 8 - dist, 0), 0.0)

    def scan_tile(sr, si, cr, ci, d):
        forward = d == 0
        pr, pi, steps = tabs[d]
        hr, hi = sr, si
        for (lr, li), dist in zip(steps, (1, 2, 4)):
            mr, mi = _cmul(lr, li, shifted(hr, dist, forward), shifted(hi, dist, forward))
            hr, hi = hr + mr, hi + mi
        mr, mi = _cmul(pr, pi, cr, ci)
        er, ei = hr + mr, hi + mi
        if forward:
            xr = jnp.where(row >= 1, pltpu.roll(er, 1, 0), cr)
            xi = jnp.where(row >= 1, pltpu.roll(ei, 1, 0), ci)
            return xr, xi, bcast(er, 7), bcast(ei, 7)
        xr = jnp.where(row < 7, pltpu.roll(er, 7, 0), cr)
        xi = jnp.where(row < 7, pltpu.roll(ei, 7, 0), ci)
        return xr, xi, bcast(er, 0), bcast(ei, 0)

    def body(i, carry):
        new = []
        for s in range(n_seq):
            for d in range(2):
                cr, ci = carry[2 * (2 * s + d)], carry[2 * (2 * s + d) + 1]
                r0 = pl.multiple_of((i if d == 0 else n_tiles - 1 - i) * 8, 8)
                sr, si = st_ref[2 * d, s, pl.ds(r0, 8), :], st_ref[2 * d + 1, s, pl.ds(r0, 8), :]
                xr, xi, ncr, nci = scan_tile(sr, si, cr, ci, d)
                xin_ref[2 * d, s, pl.ds(r0, 8), :] = xr
                xin_ref[2 * d + 1, s, pl.ds(r0, 8), :] = xi
                new += [ncr, nci]
        return tuple(new)

    zero = jnp.zeros((8, lanes), F32)
    lax.fori_loop(0, n_tiles, body, (zero,) * (4 * n_seq))


def _ssm_out_kernel(u_ref, xin_ref, mi_ref, qt_ref, o_ref):
    lanes_contract = (((1,), (1,)), ((), ()))
    acc = lax.dot_general(xin_ref[0].astype(BF16), qt_ref[0], lanes_contract, preferred_element_type=F32)
    for part in range(1, 4):
        acc = acc + lax.dot_general(xin_ref[part].astype(BF16), qt_ref[part], lanes_contract,
                                    preferred_element_type=F32)
    for e in range(2):
        y0 = jnp.dot(u_ref[e], mi_ref[e], preferred_element_type=F32)
        o_ref[e] = y0 + acc[:, SSM_CK * e:SSM_CK * (e + 1)]


def _ssm_scan(upack, batch, seq, tables, layer):
    intra, sproj, qproj, lam_pows = tables
    n_chunks = seq // SSM_CHUNK
    rows = n_chunks * batch
    n_state = SSM_GROUPS * SSM_STATE
    tr = SSM_ROW_TILE
    st = pl.pallas_call(
        _ssm_state_kernel,
        grid=(SSM_PAIRS, rows // tr),
        in_specs=[pl.BlockSpec((2, tr, SSM_CK), lambda p, r: (p, r, 0)),
                  pl.BlockSpec((None, None, 2, SSM_CK, 512), lambda p, r: (layer, p, 0, 0, 0))],
        out_specs=pl.BlockSpec((4, tr, 128), lambda p, r: (0, r, p)),
        out_shape=jax.ShapeDtypeStruct((4, rows, n_state), F32),
        compiler_params=_cparams(2),
        name="ssm_state",
    )(upack, sproj)
    lanes = SSM_SCAN_LANES
    seq_block = pl.BlockSpec((4, batch, n_chunks, lanes), lambda j: (0, 0, 0, j))
    xin = pl.pallas_call(
        _ssm_scan_kernel,
        grid=(n_state // lanes,),
        in_specs=[seq_block, pl.BlockSpec((None, 32, lanes), lambda j: (layer, 0, j))],
        out_specs=seq_block,
        out_shape=jax.ShapeDtypeStruct((4, batch, n_chunks, n_state), F32),
        compiler_params=_cparams(1),
        name="ssm_scan",
    )(st.reshape(4, batch, n_chunks, n_state), lam_pows)
    return pl.pallas_call(
        _ssm_out_kernel,
        grid=(SSM_PAIRS, rows // tr),
        in_specs=[pl.BlockSpec((2, tr, SSM_CK), lambda p, r: (p, r, 0)),
                  pl.BlockSpec((4, tr, 128), lambda p, r: (0, r, p)),
                  pl.BlockSpec((None, 2, SSM_CK, SSM_CK), lambda p, r: (layer, p, 0, 0)),
                  pl.BlockSpec((None, None, 4, 2 * SSM_CK, 128), lambda p, r: (layer, p, 0, 0, 0))],
        out_specs=pl.BlockSpec((2, tr, SSM_CK), lambda p, r: (p, r, 0)),
        out_shape=jax.ShapeDtypeStruct((SSM_GROUPS, rows, SSM_CK), F32),
        compiler_params=_cparams(2),
        name="ssm_out",
    )(upack, xin.reshape(4, rows, n_state), intra, qproj)


def _merge_kernel(x_ref, gate_ref, fm_ref, na_ref, ypack_ref, ussm_ref, dskip_ref, wglu_ref,
                  wfn_ref, wna_ref, wssm_ref, wout_ref, o_ref, ytok_ref, yrow_ref):
    y_fn = jnp.dot(fm_ref[...], wfn_ref[...], preferred_element_type=F32)
    y_na = jnp.dot(na_ref[...], wna_ref[...], preferred_element_type=F32)
    merged = (gate_ref[:, :D_MODEL].astype(F32) * y_fn
              + gate_ref[:, D_MODEL:2 * D_MODEL].astype(F32) * y_na)
    for t in range(SSM_CHUNK):
        for g in range(SSM_GROUPS):
            yrow_ref[:, g * SSM_GROUP_DIM:(g + 1) * SSM_GROUP_DIM] = (
                ypack_ref[g, :, t * SSM_GROUP_DIM:(t + 1) * SSM_GROUP_DIM])
        ytok_ref[:, t, :] = yrow_ref[...]
    yscan = ytok_ref[...].reshape(TOKEN_TILE, SSM_WIDTH)
    y = yscan + dskip_ref[...] * ussm_ref[...]
    y = jax.nn.gelu(y)
    y = y * jax.nn.sigmoid(jnp.dot(y.astype(BF16), wglu_ref[...], preferred_element_type=F32))
    y_ssm = jnp.dot(y.astype(BF16), wssm_ref[...], preferred_element_type=F32)
    merged = merged + gate_ref[:, 2 * D_MODEL:].astype(F32) * y_ssm
    o_ref[...] = x_ref[...] + jnp.dot(merged.astype(BF16), wout_ref[...], preferred_element_type=F32)


def _merge(x, gates, fm, na, ypack, ussm, ssm_d, w_glu, w_br_fn, w_br_na, w_br_ssm, w_out, layer):
    n = x.shape[0]
    tm = TOKEN_TILE
    row = lambda width: pl.BlockSpec((tm, width), lambda i: (i, 0))
    wspec = lambda k, m: _resident((None, k, m), lambda i: (layer, 0, 0))
    return pl.pallas_call(
        _merge_kernel,
        grid=(n // tm,),
        in_specs=[row(D_MODEL), row(N_BRANCH * D_MODEL), row(FN_WIDTH), row(NA_WIDTH),
                  pl.BlockSpec((SSM_GROUPS, tm // SSM_CHUNK, SSM_CK), lambda i: (0, i, 0)),
                  row(SSM_WIDTH), wspec(1, SSM_WIDTH), wspec(SSM_WIDTH, SSM_WIDTH),
                  wspec(FN_WIDTH, D_MODEL), wspec(NA_WIDTH, D_MODEL), wspec(SSM_WIDTH, D_MODEL),
                  wspec(D_MODEL, D_MODEL)],
        out_specs=row(D_MODEL),
        out_shape=jax.ShapeDtypeStruct((n, D_MODEL), F32),
        scratch_shapes=[pltpu.VMEM((tm // SSM_CHUNK, SSM_CHUNK, SSM_WIDTH), F32),
                        pltpu.VMEM((tm // SSM_CHUNK, SSM_WIDTH), F32)],
        compiler_params=_cparams(1),
        name="merge",
    )(x, gates, fm, na, ypack, ussm, ssm_d, w_glu, w_br_fn, w_br_na, w_br_ssm, w_out)


def _ffn_kernel(x_ref, g_ref, wup_ref, wdown_ref, gfin_ref, o_ref, up_ref, *, final_norm):
    x = x_ref[...]
    h = _rms(x, g_ref[...]).astype(BF16)
    for c in range(D_FF // D_MODEL):
        cols = slice(c * D_MODEL, (c + 1) * D_MODEL)
        a = jnp.maximum(jnp.dot(h, wup_ref[:, cols], preferred_element_type=F32), 0.0)
        up_ref[:, cols] = (a * a).astype(BF16)
    y = x + jnp.dot(up_ref[...], wdown_ref[...], preferred_element_type=F32)
    if final_norm:
        y = _rms(y, gfin_ref[...])
    o_ref[...] = y


def _ffn(x, g_ffn, w_up, w_down, g_final, layer, final_norm):
    n = x.shape[0]
    tm = TOKEN_TILE
    row = pl.BlockSpec((tm, D_MODEL), lambda i: (i, 0))
    return pl.pallas_call(
        functools.partial(_ffn_kernel, final_norm=final_norm),
        grid=(n // tm,),
        in_specs=[row,
                  _resident((None, 1, D_MODEL), lambda i: (layer, 0, 0)),
                  _resident((None, D_MODEL, D_FF), lambda i: (layer, 0, 0)),
                  _resident((None, D_FF, D_MODEL), lambda i: (layer, 0, 0)),
                  _resident((1, D_MODEL), lambda i: (0, 0))],
        out_specs=row,
        out_shape=jax.ShapeDtypeStruct((n, D_MODEL), F32),
        scratch_shapes=[pltpu.VMEM((tm, D_FF), BF16)],
        compiler_params=_cparams(1),
        name="ffn",
    )(x, g_ffn, w_up, w_down, g_final)


def kernel(x, g_mix, w_in, na_rpb, ssm_a_re, ssm_a_im, ssm_log_dt, ssm_b_re, ssm_b_im, ssm_c_re, ssm_c_im,
           ssm_d, w_glu, w_br_fn, w_br_na, w_br_ssm, w_out, g_ffn, w_up, w_down, g_final):
    batch, seq, _ = x.shape
    depth = w_in.shape[0]
    assert seq == DFT_N1 * DFT_N2 and seq % (GRID_W * NA_RQ) == 0 and (batch * seq) % TOKEN_TILE == 0
    rows = seq // GRID_W

    to_bf16 = lambda w: w.astype(BF16)
    w_in, w_glu, w_br_fn, w_br_na, w_br_ssm, w_out, w_up, w_down = map(
        to_bf16, (w_in, w_glu, w_br_fn, w_br_na, w_br_ssm, w_out, w_up, w_down))
    g_mix3 = g_mix.reshape(depth, 1, D_MODEL)
    g_ffn3 = g_ffn.reshape(depth, 1, D_MODEL)
    ssm_d3 = ssm_d.reshape(depth, 1, SSM_WIDTH)
    g_final2 = g_final.reshape(1, D_MODEL)
    dft_tables = _dft_tables(seq)
    na_slabs = _na_bias_slabs(na_rpb)
    na_row_masks = jnp.asarray(_na_row_masks(rows)).astype(BF16)
    ssm_tables = _ssm_tables(ssm_a_re, ssm_a_im, ssm_log_dt, ssm_b_re, ssm_b_im, ssm_c_re, ssm_c_im)

    xs = x.reshape(batch * seq, D_MODEL)
    for l in range(depth):
        ufn, q, k, v, ussm, upack, gates = _inproj(xs, g_mix3, w_in, l)
        fm = _fourier_mix(ufn, batch, seq, dft_tables)
        na = _neighbourhood_attention(q, k, v, na_slabs, na_row_masks, l, batch, seq)
        ypack = _ssm_scan(upack, batch, seq, ssm_tables, l)
        xs = _merge(xs, gates, fm, na, ypack, ussm, ssm_d3, w_glu, w_br_fn, w_br_na, w_br_ssm, w_out, l)
        xs = _ffn(xs, g_ffn3, w_up, w_down, g_final2, l, final_norm=(l == depth - 1))
    return xs.reshape(batch, seq, D_MODEL)
```

```python
import functools
import math

import numpy as np
import jax
import jax.numpy as jnp
from jax import lax
from jax.experimental import pallas as pl
from jax.experimental.pallas import tpu as pltpu

F32 = jnp.float32
BF16 = jnp.bfloat16

D_MODEL = 1024
GRID_W = 64
FN_GROUP_DIM = 64
FN_WIDTH = 256
NA_HEADS = 8
NA_HEAD_DIM = 64
NA_WIDTH = 512
NA_KR = 8
NA_KC = 16
NEG_INF = -1e30
SSM_GROUPS = 16
SSM_GROUP_DIM = 16
SSM_WIDTH = 256
SSM_STATE = 64
N_BRANCH = 3
D_FF = 4 * D_MODEL
D_IN = FN_WIDTH + 3 * NA_WIDTH + SSM_WIDTH + N_BRANCH * D_MODEL
RMS_EPS = 1e-6
LOG2_E = 1.0 / math.log(2.0)

DFT_N1 = 128
DFT_N2 = 64
NA_RQ = 4
NA_QB = NA_RQ * GRID_W
SSM_CHUNK = 16
SSM_PAIRS = SSM_GROUPS // 2
SSM_CK = SSM_CHUNK * SSM_GROUP_DIM

TOKEN_TILE = 512
INPROJ_TILE = 1024
VMEM_LIMIT = 56 * 1024 * 1024


def _cparams(n_axes):
    return pltpu.CompilerParams(dimension_semantics=("arbitrary",) * n_axes, vmem_limit_bytes=VMEM_LIMIT)


def _resident(block_shape, index_map):
    return pl.BlockSpec(block_shape, index_map, pipeline_mode=pl.Buffered(1))


def _rms(x, g):
    ms = jnp.mean(x * x, axis=-1, keepdims=True)
    return x * lax.rsqrt(ms + RMS_EPS) * g


def _inproj_kernel(x_ref, g_ref, w_ref, ufn_ref, q_ref, k_ref, v_ref, ussm_ref, upack_ref, gate_ref):
    h = _rms(x_ref[...], g_ref[...]).astype(BF16)

    def seg(lo, width):
        return jnp.dot(h, w_ref[:, lo:lo + width], preferred_element_type=F32)

    ufn_ref[...] = seg(0, FN_WIDTH)
    q_ref[...] = (seg(FN_WIDTH, NA_WIDTH) * (NA_HEAD_DIM ** -0.5 * LOG2_E)).astype(BF16)
    k_ref[...] = seg(FN_WIDTH + NA_WIDTH, NA_WIDTH).astype(BF16)
    v_ref[...] = seg(FN_WIDTH + 2 * NA_WIDTH, NA_WIDTH).astype(BF16)
    ussm = seg(FN_WIDTH + 3 * NA_WIDTH, SSM_WIDTH)
    ussm_ref[...] = ussm
    u3 = ussm.astype(BF16).reshape(INPROJ_TILE // SSM_CHUNK, SSM_CHUNK, SSM_WIDTH)
    for t in range(SSM_CHUNK):
        ut = u3[:, t, :]
        for g in range(SSM_GROUPS):
            upack_ref[g, :, t * SSM_GROUP_DIM:(t + 1) * SSM_GROUP_DIM] = (
                ut[:, g * SSM_GROUP_DIM:(g + 1) * SSM_GROUP_DIM])
    gate_lo = FN_WIDTH + 3 * NA_WIDTH + SSM_WIDTH
    for j in range(N_BRANCH):
        gate_ref[:, j * D_MODEL:(j + 1) * D_MODEL] = jax.nn.sigmoid(seg(gate_lo + j * D_MODEL, D_MODEL)).astype(BF16)


def _inproj(x, g_mix, w_in, layer):
    n = x.shape[0]
    tm = INPROJ_TILE
    row = lambda width: pl.BlockSpec((tm, width), lambda i: (i, 0))
    return pl.pallas_call(
        _inproj_kernel,
        grid=(n // tm,),
        in_specs=[row(D_MODEL),
                  _resident((None, 1, D_MODEL), lambda i: (layer, 0, 0)),
                  _resident((None, D_MODEL, D_IN), lambda i: (layer, 0, 0))],
        out_specs=[row(FN_WIDTH), row(NA_WIDTH), row(NA_WIDTH), row(NA_WIDTH), row(SSM_WIDTH),
                   pl.BlockSpec((SSM_GROUPS, tm // SSM_CHUNK, SSM_CK), lambda i: (0, i, 0)),
                   row(N_BRANCH * D_MODEL)],
        out_shape=[jax.ShapeDtypeStruct((n, FN_WIDTH), F32),
                   jax.ShapeDtypeStruct((n, NA_WIDTH), BF16),
                   jax.ShapeDtypeStruct((n, NA_WIDTH), BF16),
                   jax.ShapeDtypeStruct((n, NA_WIDTH), BF16),
                   jax.ShapeDtypeStruct((n, SSM_WIDTH), F32),
                   jax.ShapeDtypeStruct((SSM_GROUPS, n // SSM_CHUNK, SSM_CK), BF16),
                   jax.ShapeDtypeStruct((n, N_BRANCH * D_MODEL), BF16)],
        compiler_params=_cparams(1),
        name="inproj",
    )(x, g_mix, w_in)


def _dft_tables(seq):
    two_pi = 2.0 * math.pi
    k1 = lax.broadcasted_iota(jnp.int32, (DFT_N1, DFT_N1), 0)
    n1 = lax.broadcasted_iota(jnp.int32, (DFT_N1, DFT_N1), 1)
    ang = ((k1 * n1) % DFT_N1).astype(F32) * (two_pi / DFT_N1)
    wa = jnp.concatenate([jnp.cos(ang), -jnp.sin(ang)], axis=0)
    wa = jnp.kron(wa, jnp.eye(DFT_A_N2, dtype=F32))

    shape = (DFT_N1, DFT_N2, DFT_N2)
    kk1 = lax.broadcasted_iota(jnp.int32, shape, 0)
    kk2 = lax.broadcasted_iota(jnp.int32, shape, 1)
    nn2 = lax.broadcasted_iota(jnp.int32, shape, 2)
    ang = ((nn2 * (kk1 + DFT_N1 * kk2)) % seq).astype(F32) * (two_pi / seq)
    c, s = jnp.cos(ang), jnp.sin(ang)
    g = jnp.concatenate([jnp.concatenate([c, s], axis=2), jnp.concatenate([-s, c], axis=2)], axis=1)

    m = lax.broadcasted_iota(jnp.int32, (FN_WIDTH, FN_WIDTH), 0)
    j = lax.broadcasted_iota(jnp.int32, (FN_WIDTH, FN_WIDTH), 1)
    same_group = (m // FN_GROUP_DIM) == (j // FN_GROUP_DIM)
    ang = ((m * j) % FN_GROUP_DIM).astype(F32) * (two_pi / FN_GROUP_DIM)
    scale = 1.0 / math.sqrt(seq * FN_GROUP_DIM)
    cc = jnp.where(same_group, jnp.cos(ang) * scale, 0.0)
    sc = jnp.where(same_group, jnp.sin(ang) * scale, 0.0)
    return wa.astype(BF16), g.astype(BF16), cc.astype(BF16), sc.astype(BF16)


DFT_A_N2 = 8
DFT_C_K1 = 16


def _dft_a_kernel(w_ref, x_ref, o_ref):
    x = x_ref[...].reshape(DFT_N1 * DFT_A_N2, FN_WIDTH).astype(BF16)
    a = jnp.dot(w_ref[...], x, preferred_element_type=F32)
    o_ref[...] = a.reshape(2, DFT_N1, DFT_A_N2, FN_WIDTH)


def _dft_c_kernel(ar_ref, ai_ref, g_ref, cc_ref, sc_ref, o_ref, zr_ref, zi_ref):
    for j in range(DFT_C_K1):
        a = jnp.concatenate([ar_ref[j], ai_ref[j]], axis=0).astype(BF16)
        z = jnp.dot(g_ref[j], a, preferred_element_type=F32)
        zr_ref[j * DFT_N2:(j + 1) * DFT_N2, :] = z[:DFT_N2].astype(BF16)
        zi_ref[j * DFT_N2:(j + 1) * DFT_N2, :] = z[DFT_N2:].astype(BF16)
    o = (jnp.dot(zr_ref[...], cc_ref[...], preferred_element_type=F32)
         + jnp.dot(zi_ref[...], sc_ref[...], preferred_element_type=F32)).astype(BF16)
    for j in range(DFT_C_K1):
        o_ref[:, j, :] = o[j * DFT_N2:(j + 1) * DFT_N2]


def _fourier_mix(ufn, batch, seq, tables):
    wa, g, cc, sc = tables
    x = ufn.reshape(batch, DFT_N1, DFT_N2, FN_WIDTH)
    nb = DFT_A_N2
    a = pl.pallas_call(
        _dft_a_kernel,
        grid=(batch, DFT_N2 // nb),
        in_specs=[_resident((2 * DFT_N1 * nb, DFT_N1 * nb), lambda b, j: (0, 0)),
                  pl.BlockSpec((None, DFT_N1, nb, FN_WIDTH), lambda b, j: (b, 0, j, 0))],
        out_specs=pl.BlockSpec((None, 2, DFT_N1, nb, FN_WIDTH), lambda b, j: (b, 0, 0, j, 0)),
        out_shape=jax.ShapeDtypeStruct((batch, 2, DFT_N1, DFT_N2, FN_WIDTH), F32),
        compiler_params=_cparams(2),
        name="dft_a",
    )(wa, x)
    kb = DFT_C_K1
    part = lambda p: pl.BlockSpec((None, None, kb, DFT_N2, FN_WIDTH), lambda b, i: (b, p, i, 0, 0))
    out = pl.pallas_call(
        _dft_c_kernel,
        grid=(batch, DFT_N1 // kb),
        in_specs=[part(0), part(1),
                  pl.BlockSpec((kb, 2 * DFT_N2, 2 * DFT_N2), lambda b, i: (i, 0, 0)),
                  _resident((FN_WIDTH, FN_WIDTH), lambda b, i: (0, 0)),
                  _resident((FN_WIDTH, FN_WIDTH), lambda b, i: (0, 0))],
        out_specs=pl.BlockSpec((None, DFT_N2, kb, FN_WIDTH), lambda b, i: (b, 0, i, 0)),
        out_shape=jax.ShapeDtypeStruct((batch, DFT_N2, DFT_N1, FN_WIDTH), BF16),
        scratch_shapes=[pltpu.VMEM((kb * DFT_N2, FN_WIDTH), BF16), pltpu.VMEM((kb * DFT_N2, FN_WIDTH), BF16)],
        compiler_params=_cparams(2),
        name="dft_c",
    )(a, a, g, cc, sc)
    return out.reshape(batch * seq, FN_WIDTH)


NA_DR = 2 * NA_KR - 1


def _na_bias_slabs(rpb):
    depth = rpb.shape[0]
    pad = GRID_W - NA_KC
    period = 2 * GRID_W - 1
    ext = jnp.pad(rpb * LOG2_E, ((0, 0), (0, 0), (0, 0), (pad, pad)))
    flat = jnp.tile(ext, (1, 1, 1, GRID_W))[..., GRID_W - 1:GRID_W - 1 + GRID_W * (period - 1)]
    slab = flat.reshape(depth, NA_HEADS, NA_DR, GRID_W, period - 1)[..., :GRID_W]
    qc, kc = np.arange(GRID_W)[:, None], np.arange(GRID_W)[None, :]
    col_start = np.clip(qc - NA_KC // 2, 0, GRID_W - NA_KC)
    col_ok = (kc >= col_start) & (kc < col_start + NA_KC)
    slab = jnp.where(col_ok, slab, NEG_INF)
    return jnp.concatenate([slab[:, :, :-1], slab[:, :, 1:]], axis=-1)


def _na_row_masks(rows):
    kr = min(NA_KR, rows)
    n_blocks = rows // NA_RQ
    out = np.zeros((3, 2, 3, NA_QB, 2 * NA_HEAD_DIM), np.float32)
    for v, blk in enumerate((0, 1, n_blocks - 1)):
        r = blk * NA_RQ + np.arange(NA_RQ)[:, None]
        key_row = (blk - 1) * NA_RQ + np.arange(3 * NA_RQ)[None, :]
        row_start = np.clip(r - kr // 2, 0, rows - kr)
        row_ok = (key_row >= row_start) & (key_row < row_start + kr)
        pen = np.where(row_ok, 0.0, NEG_INF).reshape(NA_RQ, 3, NA_RQ)
        pen = np.repeat(pen.transpose(1, 2, 0), GRID_W, axis=1)
        for e in range(2):
            lo = NA_HEAD_DIM * (1 - e)
            out[v, e, :, :, lo:lo + NA_RQ] = pen
    return out


def _na_kernel(q_ref, kp_ref, kc_ref, kn_ref, vp_ref, vc_ref, vn_ref, slab_ref, rmask_ref, o_ref):
    lane = lax.broadcasted_iota(jnp.int32, (1, 2 * NA_HEAD_DIM), 1)
    first = lane < NA_HEAD_DIM
    q_row = lax.broadcasted_iota(jnp.int32, (NA_QB, 2 * NA_HEAD_DIM), 0) // GRID_W
    q_lane = lax.broadcasted_iota(jnp.int32, (NA_QB, 2 * NA_HEAD_DIM), 1)
    k_refs = (kp_ref, kc_ref, kn_ref)
    v_refs = (vp_ref, vc_ref, vn_ref)
    for hp in range(NA_HEADS // 2):
        sl = slice(2 * NA_HEAD_DIM * hp, 2 * NA_HEAD_DIM * (hp + 1))
        q2 = q_ref[:, sl]
        acc = []
        for e in range(2):
            mine = first if e == 0 else jnp.logical_not(first)
            row_onehot = (q_lane == NA_HEAD_DIM * (1 - e) + q_row).astype(BF16)
            qm = jnp.where(mine, q2, row_onehot)
            ms, rs = [], []
            for i in range(3):
                km = jnp.where(mine, k_refs[i][:, sl], rmask_ref[e, i])
                s = lax.dot_general(qm, km, (((1,), (1,)), ((), ())), preferred_element_type=F32)
                bias = jnp.concatenate(
                    [jnp.concatenate([slab_ref[2 * hp + e, NA_RQ * i + 2 * m - a + NA_KR - 1 - NA_RQ]
                                      for m in range(NA_RQ // 2)], axis=1) for a in range(NA_RQ)], axis=0)
                s = s + bias
                m = jnp.max(s, axis=-1, keepdims=True)
                p = jnp.exp2(s - m).astype(BF16)
                v2 = v_refs[i][:, sl]
                vm = jnp.where(mine, v2, jnp.ones_like(v2))
                ms.append(m)
                rs.append(jnp.dot(p, vm, preferred_element_type=F32))
            m_all = jnp.maximum(jnp.maximum(ms[0], ms[1]), ms[2])
            acc.append(rs[0] * jnp.exp2(ms[0] - m_all) + rs[1] * jnp.exp2(ms[1] - m_all)
                       + rs[2] * jnp.exp2(ms[2] - m_all))
        num = jnp.where(first, acc[0], acc[1])
        den = pltpu.roll(jnp.where(first, acc[1], acc[0]), NA_HEAD_DIM, 1)
        o_ref[:, sl] = (num / den).astype(BF16)


def _neighbourhood_attention(q, k, v, slabs, row_masks, layer, batch, seq):
    n_blocks = seq // NA_QB
    q3, k3, v3 = (t.reshape(batch, seq, NA_WIDTH) for t in (q, k, v))
    blk = lambda shift: pl.BlockSpec(
        (None, NA_QB, NA_WIDTH), lambda b, i: (b, jnp.clip(i + shift, 0, n_blocks - 1), 0))
    variant = lambda b, i: (jnp.where(i == 0, 0, jnp.where(i == n_blocks - 1, 2, 1)), 0, 0, 0, 0)
    out = pl.pallas_call(
        _na_kernel,
        grid=(batch, n_blocks),
        in_specs=[blk(0), blk(-1), blk(0), blk(1), blk(-1), blk(0), blk(1),
                  _resident((None, NA_HEADS, NA_DR - 1, GRID_W, 2 * GRID_W), lambda b, i: (layer, 0, 0, 0, 0)),
                  pl.BlockSpec((None, 2, 3, NA_QB, 2 * NA_HEAD_DIM), variant)],
        out_specs=blk(0),
        out_shape=jax.ShapeDtypeStruct((batch, seq, NA_WIDTH), BF16),
        compiler_params=_cparams(2),
        name="natten",
    )(q3, k3, k3, k3, v3, v3, v3, slabs, row_masks)
    return out.reshape(batch * seq, NA_WIDTH)


def _cmul(ar, ai, br, bi):
    return ar * br - ai * bi, ar * bi + ai * br


def _ssm_table_kernel(pr_ref, pi_ref, bbr_ref, bbi_ref, cr_ref, ci_ref, intra_ref, sp_ref, qt_ref):
    n, c = SSM_CHUNK, SSM_GROUP_DIM
    hi = lax.Precision.HIGHEST
    lanes_contract = (((1,), (1,)), ((), ()))
    sp_ref[...] = jnp.zeros(sp_ref.shape, sp_ref.dtype)
    qt_ref[...] = jnp.zeros(qt_ref.shape, qt_ref.dtype)
    spread = (lax.broadcasted_iota(jnp.int32, (c, SSM_CK), 1) % c
              == lax.broadcasted_iota(jnp.int32, (c, SSM_CK), 0)).astype(F32)
    delta = (lax.broadcasted_iota(jnp.int32, (SSM_CK, SSM_CK), 1) // c
             - lax.broadcasted_iota(jnp.int32, (SSM_CK, SSM_CK), 0) // c)
    for e in range(2):
        toeplitz = []
        for d in range(2):
            pr, pi = pr_ref[d, e], pi_ref[d, e]
            w = [_cmul(bbr_ref[d, e], bbi_ref[d, e], pr[t:t + 1], pi[t:t + 1]) for t in range(n)]
            wr = jnp.concatenate([x[0] for x in w], axis=0)
            wi = jnp.concatenate([x[1] for x in w], axis=0)
            kt = (lax.dot_general(wr, cr_ref[d, e], lanes_contract, precision=hi, preferred_element_type=F32)
                  - lax.dot_general(wi, ci_ref[d, e], lanes_contract, precision=hi, preferred_element_type=F32))
            toeplitz.append(jnp.dot(kt, spread, precision=hi, preferred_element_type=F32))
            order = range(n - 1, -1, -1) if d == 0 else range(n)
            for ri in range(2):
                col = (2 * d + ri) * 2 * SSM_STATE + e * SSM_STATE
                sp_ref[e, :, col:col + SSM_STATE] = jnp.concatenate([w[t][ri] for t in order], axis=0).astype(BF16)
            exps = range(1, n + 1) if d == 0 else range(n, 0, -1)
            q = [_cmul(cr_ref[d, e], ci_ref[d, e], pr[m:m + 1], pi[m:m + 1]) for m in exps]
            rows = slice(e * SSM_CK, (e + 1) * SSM_CK)
            cols = slice(e * SSM_STATE, (e + 1) * SSM_STATE)
            qt_ref[2 * d, rows, cols] = jnp.concatenate([x[0] for x in q], axis=0).astype(BF16)
            qt_ref[2 * d + 1, rows, cols] = (-jnp.concatenate([x[1] for x in q], axis=0)).astype(BF16)
        fwd, bwd = toeplitz
        acc = jnp.zeros((SSM_CK, SSM_CK), F32)
        for dlt in range(-(n - 1), n):
            if dlt > 0:
                src = fwd[dlt * c:(dlt + 1) * c]
            elif dlt < 0:
                src = bwd[-dlt * c:(-dlt + 1) * c]
            else:
                src = fwd[:c] + bwd[:c]
            acc = jnp.where(delta == dlt, jnp.concatenate([src] * n, axis=0), acc)
        intra_ref[e] = acc.astype(BF16)


def _ssm_tables(a_re, a_im, log_dt, b_re, b_im, c_re, c_im):
    depth = a_re.shape[0]
    dt = jnp.exp(log_dt)[..., None]
    zr, zi = a_re * dt, a_im * dt

    def powers(taus):
        taus = taus.astype(F32)[:, None]
        mag = jnp.exp(zr[..., None, :] * taus)
        return mag * jnp.cos(zi[..., None, :] * taus), mag * jnp.sin(zi[..., None, :] * taus)

    pr, pi = powers(jnp.arange(SSM_CHUNK + 1))
    lbr, lbi = pr[..., 1, :], pi[..., 1, :]
    den = a_re * a_re + a_im * a_im
    fr = ((lbr - 1.0) * a_re + lbi * a_im) / den
    fi = (lbi * a_re - (lbr - 1.0) * a_im) / den
    bbr, bbi = _cmul(fr[..., None, :], fi[..., None, :],
                     jnp.swapaxes(b_re, -1, -2), jnp.swapaxes(b_im, -1, -2))

    pair = lambda rows: pl.BlockSpec((None, 2, 2, rows, SSM_STATE), lambda l, p: (l, 0, p, 0, 0))
    intra, sproj, qproj_t = pl.pallas_call(
        _ssm_table_kernel,
        grid=(depth, SSM_PAIRS),
        in_specs=[pair(SSM_CHUNK + 1), pair(SSM_CHUNK + 1), pair(SSM_GROUP_DIM), pair(SSM_GROUP_DIM),
                  pair(SSM_GROUP_DIM), pair(SSM_GROUP_DIM)],
        out_specs=[pl.BlockSpec((None, 2, SSM_CK, SSM_CK), lambda l, p: (l, p, 0, 0)),
                   pl.BlockSpec((None, None, 2, SSM_CK, 8 * SSM_STATE), lambda l, p: (l, p, 0, 0, 0)),
                   pl.BlockSpec((None, None, 4, 2 * SSM_CK, 2 * SSM_STATE), lambda l, p: (l, p, 0, 0, 0))],
        out_shape=[jax.ShapeDtypeStruct((depth, SSM_GROUPS, SSM_CK, SSM_CK), BF16),
                   jax.ShapeDtypeStruct((depth, SSM_PAIRS, 2, SSM_CK, 8 * SSM_STATE), BF16),
                   jax.ShapeDtypeStruct((depth, SSM_PAIRS, 4, 2 * SSM_CK, 2 * SSM_STATE), BF16)],
        compiler_params=_cparams(2),
        name="ssm_tables",
    )(pr, pi, bbr, bbi, c_re, c_im)

    sr, si = powers(jnp.arange(1, 9) * SSM_CHUNK)
    sr, si = sr.transpose(0, 1, 3, 2, 4), si.transpose(0, 1, 3, 2, 4)
    lam_pows = jnp.stack([jnp.stack([sr[:, 0], si[:, 0]], axis=1),
                          jnp.stack([sr[:, 1, ::-1], si[:, 1, ::-1]], axis=1)], axis=1)
    lam_pows = lam_pows.reshape(depth, 4 * 8, SSM_GROUPS * SSM_STATE)
    return intra, sproj, qproj_t, lam_pows


SSM_ROW_TILE = 1024
SSM_SCAN_LANES = 128


def _ssm_state_kernel(u_ref, sp_ref, st_ref):
    st = (jnp.dot(u_ref[0], sp_ref[0], preferred_element_type=F32)
          + jnp.dot(u_ref[1], sp_ref[1], preferred_element_type=F32))
    for part in range(4):
        st_ref[part] = st[:, 128 * part:128 * (part + 1)]


def _ssm_scan_kernel(st_ref, lam_ref, xin_ref):
    n_seq, n_tiles, lanes = st_ref.shape[1], st_ref.shape[2] // 8, st_ref.shape[3]
    row = lax.broadcasted_iota(jnp.int32, (8, lanes), 0)

    def bcast(x, r):
        return jnp.broadcast_to(x[r:r + 1, :], (8, lanes))

    tabs = []
    for d in range(2):
        pr, pi = lam_ref[16 * d:16 * d + 8, :], lam_ref[16 * d + 8:16 * d + 16, :]
        at = (lambda m: m - 1) if d == 0 else (lambda m: 8 - m)
        tabs.append((pr, pi, [(bcast(pr, at(m)), bcast(pi, at(m))) for m in (1, 2, 4)]))

    def shifted(x, dist, forward):
        if forward:
            return jnp.where(row >= dist, pltpu.roll(x, dist, 0), 0.0)
        return jnp.where(row < 8 - dist, pltpu.roll(x, 8 - dist, 0), 0.0)

    def scan_tile(sr, si, cr, ci, d):
        forward = d == 0
        pr, pi, steps = tabs[d]
        hr, hi = sr, si
        for (lr, li), dist in zip(steps, (1, 2, 4)):
            mr, mi = _cmul(lr, li, shifted(hr, dist, forward), shifted(hi, dist, forward))
            hr, hi = hr + mr, hi + mi
        mr, mi = _cmul(pr, pi, cr, ci)
        er, ei = hr + mr, hi + mi
        if forward:
            xr = jnp.where(row >= 1, pltpu.roll(er, 1, 0), cr)
            xi = jnp.where(row >= 1, pltpu.roll(ei, 1, 0), ci)
            return xr, xi, bcast(er, 7), bcast(ei, 7)
        xr = jnp.where(row < 7, pltpu.roll(er, 7, 0), cr)
        xi = jnp.where(row < 7, pltpu.roll(ei, 7, 0), ci)
        return xr, xi, bcast(er, 0), bcast(ei, 0)

    def body(i, carry):
        new = []
        for s in range(n_seq):
            for d in range(2):
                cr, ci = carry[2 * (2 * s + d)], carry[2 * (2 * s + d) + 1]
                r0 = pl.multiple_of((i if d == 0 else n_tiles - 1 - i) * 8, 8)
                sr, si = st_ref[2 * d, s, pl.ds(r0, 8), :], st_ref[2 * d + 1, s, pl.ds(r0, 8), :]
                xr, xi, ncr, nci = scan_tile(sr, si, cr, ci, d)
                xin_ref[2 * d, s, pl.ds(r0, 8), :] = xr
                xin_ref[2 * d + 1, s, pl.ds(r0, 8), :] = xi
                new += [ncr, nci]
        return tuple(new)

    zero = jnp.zeros((8, lanes), F32)
    lax.fori_loop(0, n_tiles, body, (zero,) * (4 * n_seq))


def _ssm_out_kernel(u_ref, xin_ref, mi_ref, qt_ref, o_ref):
    lanes_contract = (((1,), (1,)), ((), ()))
    acc = lax.dot_general(xin_ref[0].astype(BF16), qt_ref[0], lanes_contract, preferred_element_type=F32)
    for part in range(1, 4):
        acc = acc + lax.dot_general(xin_ref[part].astype(BF16), qt_ref[part], lanes_contract,
                                    preferred_element_type=F32)
    for e in range(2):
        y0 = jnp.dot(u_ref[e], mi_ref[e], preferred_element_type=F32)
        o_ref[e] = y0 + acc[:, SSM_CK * e:SSM_CK * (e + 1)]


def _ssm_scan(upack, batch, seq, tables, layer):
    intra, sproj, qproj, lam_pows = tables
    n_chunks = seq // SSM_CHUNK
    rows = n_chunks * batch
    n_state = SSM_GROUPS * SSM_STATE
    tr = SSM_ROW_TILE
    st = pl.pallas_call(
        _ssm_state_kernel,
        grid=(SSM_PAIRS, rows // tr),
        in_specs=[pl.BlockSpec((2, tr, SSM_CK), lambda p, r: (p, r, 0)),
                  pl.BlockSpec((None, None, 2, SSM_CK, 512), lambda p, r: (layer, p, 0, 0, 0))],
        out_specs=pl.BlockSpec((4, tr, 128), lambda p, r: (0, r, p)),
        out_shape=jax.ShapeDtypeStruct((4, rows, n_state), F32),
        compiler_params=_cparams(2),
        name="ssm_state",
    )(upack, sproj)
    lanes = SSM_SCAN_LANES
    seq_block = pl.BlockSpec((4, batch, n_chunks, lanes), lambda j: (0, 0, 0, j))
    xin = pl.pallas_call(
        _ssm_scan_kernel,
        grid=(n_state // lanes,),
        in_specs=[seq_block, pl.BlockSpec((None, 32, lanes), lambda j: (layer, 0, j))],
        out_specs=seq_block,
        out_shape=jax.ShapeDtypeStruct((4, batch, n_chunks, n_state), F32),
        compiler_params=_cparams(1),
        name="ssm_scan",
    )(st.reshape(4, batch, n_chunks, n_state), lam_pows)
    return pl.pallas_call(
        _ssm_out_kernel,
        grid=(SSM_PAIRS, rows // tr),
        in_specs=[pl.BlockSpec((2, tr, SSM_CK), lambda p, r: (p, r, 0)),
                  pl.BlockSpec((4, tr, 128), lambda p, r: (0, r, p)),
                  pl.BlockSpec((None, 2, SSM_CK, SSM_CK), lambda p, r: (layer, p, 0, 0)),
                  pl.BlockSpec((None, None, 4, 2 * SSM_CK, 128), lambda p, r: (layer, p, 0, 0, 0))],
        out_specs=pl.BlockSpec((2, tr, SSM_CK), lambda p, r: (p, r, 0)),
        out_shape=jax.ShapeDtypeStruct((SSM_GROUPS, rows, SSM_CK), F32),
        compiler_params=_cparams(2),
        name="ssm_out",
    )(upack, xin.reshape(4, rows, n_state), intra, qproj)


def _merge_tile(x_ref, gate_ref, fm_ref, na_ref, ypack_ref, ussm_ref, dskip_ref, wglu_ref,
                wfn_ref, wna_ref, wssm_ref, wout_ref, ytok_ref, yrow_ref):
    y_fn = jnp.dot(fm_ref[...], wfn_ref[...], preferred_element_type=F32)
    y_na = jnp.dot(na_ref[...], wna_ref[...], preferred_element_type=F32)
    merged = (gate_ref[:, :D_MODEL].astype(F32) * y_fn
              + gate_ref[:, D_MODEL:2 * D_MODEL].astype(F32) * y_na)
    for t in range(SSM_CHUNK):
        for g in range(SSM_GROUPS):
            yrow_ref[:, g * SSM_GROUP_DIM:(g + 1) * SSM_GROUP_DIM] = (
                ypack_ref[g, :, t * SSM_GROUP_DIM:(t + 1) * SSM_GROUP_DIM])
        ytok_ref[:, t, :] = yrow_ref[...]
    yscan = ytok_ref[...].reshape(TOKEN_TILE, SSM_WIDTH)
    y = yscan + dskip_ref[...] * ussm_ref[...]
    y = jax.nn.gelu(y)
    y = y * jax.nn.sigmoid(jnp.dot(y.astype(BF16), wglu_ref[...], preferred_element_type=F32))
    y_ssm = jnp.dot(y.astype(BF16), wssm_ref[...], preferred_element_type=F32)
    merged = merged + gate_ref[:, 2 * D_MODEL:].astype(F32) * y_ssm
    return x_ref[...] + jnp.dot(merged.astype(BF16), wout_ref[...], preferred_element_type=F32)


def _ffn_tile(x, g_ref, wup_ref, wdown_ref, gfin_ref, up_ref, final_norm):
    h = _rms(x, g_ref[...]).astype(BF16)
    for c in range(D_FF // D_MODEL):
        cols = slice(c * D_MODEL, (c + 1) * D_MODEL)
        a = jnp.maximum(jnp.dot(h, wup_ref[:, cols], preferred_element_type=F32), 0.0)
        up_ref[:, cols] = (a * a).astype(BF16)
    y = x + jnp.dot(up_ref[...], wdown_ref[...], preferred_element_type=F32)
    if final_norm:
        y = _rms(y, gfin_ref[...])
    return y


def _mix_ffn_kernel(x_ref, gate_ref, fm_ref, na_ref, ypack_ref, ussm_ref, dskip_ref, wglu_ref,
                    wfn_ref, wna_ref, wssm_ref, wout_ref, g_ref, wup_ref, wdown_ref, gfin_ref,
                    o_ref, ytok_ref, yrow_ref, up_ref, *, final_norm):
    x1 = _merge_tile(x_ref, gate_ref, fm_ref, na_ref, ypack_ref, ussm_ref, dskip_ref, wglu_ref,
                     wfn_ref, wna_ref, wssm_ref, wout_ref, ytok_ref, yrow_ref)
    o_ref[...] = _ffn_tile(x1, g_ref, wup_ref, wdown_ref, gfin_ref, up_ref, final_norm)


def _mix_ffn(x, gates, fm, na, ypack, ussm, ssm_d, w_glu, w_br_fn, w_br_na, w_br_ssm, w_out,
             g_ffn, w_up, w_down, g_final, layer, final_norm):
    n = x.shape[0]
    tm = TOKEN_TILE
    row = lambda width: pl.BlockSpec((tm, width), lambda i: (i, 0))
    wspec = lambda k, m: _resident((None, k, m), lambda i: (layer, 0, 0))
    return pl.pallas_call(
        functools.partial(_mix_ffn_kernel, final_norm=final_norm),
        grid=(n // tm,),
        in_specs=[row(D_MODEL), row(N_BRANCH * D_MODEL), row(FN_WIDTH), row(NA_WIDTH),
                  pl.BlockSpec((SSM_GROUPS, tm // SSM_CHUNK, SSM_CK), lambda i: (0, i, 0)),
                  row(SSM_WIDTH), wspec(1, SSM_WIDTH), wspec(SSM_WIDTH, SSM_WIDTH),
                  wspec(FN_WIDTH, D_MODEL), wspec(NA_WIDTH, D_MODEL), wspec(SSM_WIDTH, D_MODEL),
                  wspec(D_MODEL, D_MODEL),
                  wspec(1, D_MODEL), wspec(D_MODEL, D_FF), wspec(D_FF, D_MODEL),
                  _resident((1, D_MODEL), lambda i: (0, 0))],
        out_specs=row(D_MODEL),
        out_shape=jax.ShapeDtypeStruct((n, D_MODEL), F32),
        scratch_shapes=[pltpu.VMEM((tm // SSM_CHUNK, SSM_CHUNK, SSM_WIDTH), F32),
                        pltpu.VMEM((tm // SSM_CHUNK, SSM_WIDTH), F32),
                        pltpu.VMEM((tm, D_FF), BF16)],
        compiler_params=_cparams(1),
        name="mix_ffn",
    )(x, gates, fm, na, ypack, ussm, ssm_d, w_glu, w_br_fn, w_br_na, w_br_ssm, w_out,
      g_ffn, w_up, w_down, g_final)


def kernel(x, g_mix, w_in, na_rpb, ssm_a_re, ssm_a_im, ssm_log_dt, ssm_b_re, ssm_b_im, ssm_c_re, ssm_c_im,
           ssm_d, w_glu, w_br_fn, w_br_na, w_br_ssm, w_out, g_ffn, w_up, w_down, g_final):
    batch, seq, _ = x.shape
    depth = w_in.shape[0]
    assert seq == DFT_N1 * DFT_N2 and seq % (GRID_W * NA_RQ) == 0 and (batch * seq) % TOKEN_TILE == 0
    rows = seq // GRID_W

    to_bf16 = lambda w: w.astype(BF16)
    w_in, w_glu, w_br_fn, w_br_na, w_br_ssm, w_out, w_up, w_down = map(
        to_bf16, (w_in, w_glu, w_br_fn, w_br_na, w_br_ssm, w_out, w_up, w_down))
    g_mix3 = g_mix.reshape(depth, 1, D_MODEL)
    g_ffn3 = g_ffn.reshape(depth, 1, D_MODEL)
    ssm_d3 = ssm_d.reshape(depth, 1, SSM_WIDTH)
    g_final2 = g_final.reshape(1, D_MODEL)
    dft_tables = _dft_tables(seq)
    na_slabs = _na_bias_slabs(na_rpb)
    na_row_masks = jnp.asarray(_na_row_masks(rows)).astype(BF16)
    ssm_tables = _ssm_tables(ssm_a_re, ssm_a_im, ssm_log_dt, ssm_b_re, ssm_b_im, ssm_c_re, ssm_c_im)

    xs = x.reshape(batch * seq, D_MODEL)
    for l in range(depth):
        ufn, q, k, v, ussm, upack, gates = _inproj(xs, g_mix3, w_in, l)
        fm = _fourier_mix(ufn, batch, seq, dft_tables)
        na = _neighbourhood_attention(q, k, v, na_slabs, na_row_masks, l, batch, seq)
        ypack = _ssm_scan(upack, batch, seq, ssm_tables, l)
        xs = _mix_ffn(xs, gates, fm, na, ypack, ussm, ssm_d3, w_glu, w_br_fn, w_br_na, w_br_ssm, w_out,
                      g_ffn3, w_up, w_down, g_final2, l, final_norm=(l == depth - 1))
    return xs.reshape(batch, seq, D_MODEL)
```

```python
import functools
import math

import numpy as np
import jax
import jax.numpy as jnp
from jax import lax
from jax.experimental import pallas as pl
from jax.experimental.pallas import tpu as pltpu

F32 = jnp.float32
BF16 = jnp.bfloat16

D_MODEL = 1024
GRID_W = 64
FN_GROUP_DIM = 64
FN_WIDTH = 256
NA_HEADS = 8
NA_HEAD_DIM = 64
NA_WIDTH = 512
NA_KR = 8
NA_KC = 16
NEG_INF = -1e30
SSM_GROUPS = 16
SSM_GROUP_DIM = 16
SSM_WIDTH = 256
SSM_STATE = 64
N_BRANCH = 3
D_FF = 4 * D_MODEL
D_IN = FN_WIDTH + 3 * NA_WIDTH + SSM_WIDTH + N_BRANCH * D_MODEL
RMS_EPS = 1e-6
LOG2_E = 1.0 / math.log(2.0)

DFT_N1 = 128
DFT_N2 = 64
NA_RQ = 4
NA_QB = NA_RQ * GRID_W
SSM_CHUNK = 16
SSM_PAIRS = SSM_GROUPS // 2
SSM_CK = SSM_CHUNK * SSM_GROUP_DIM

TOKEN_TILE = 512
INPROJ_TILE = 1024
VMEM_LIMIT = 56 * 1024 * 1024


def _cparams(n_axes):
    return pltpu.CompilerParams(dimension_semantics=("arbitrary",) * n_axes, vmem_limit_bytes=VMEM_LIMIT)


def _resident(block_shape, index_map):
    return pl.BlockSpec(block_shape, index_map, pipeline_mode=pl.Buffered(1))


def _rms(x, g):
    ms = jnp.mean(x * x, axis=-1, keepdims=True)
    return x * lax.rsqrt(ms + RMS_EPS) * g


def _inproj_kernel(x_ref, g_ref, w_ref, ufn_ref, q_ref, k_ref, v_ref, ussm_ref, upack_ref, gate_ref):
    h = _rms(x_ref[...], g_ref[...]).astype(BF16)

    def seg(lo, width):
        return jnp.dot(h, w_ref[:, lo:lo + width], preferred_element_type=F32)

    ufn_ref[...] = seg(0, FN_WIDTH)
    q_ref[...] = (seg(FN_WIDTH, NA_WIDTH) * (NA_HEAD_DIM ** -0.5 * LOG2_E)).astype(BF16)
    k_ref[...] = seg(FN_WIDTH + NA_WIDTH, NA_WIDTH).astype(BF16)
    v_ref[...] = seg(FN_WIDTH + 2 * NA_WIDTH, NA_WIDTH).astype(BF16)
    ussm = seg(FN_WIDTH + 3 * NA_WIDTH, SSM_WIDTH)
    ussm_ref[...] = ussm
    u3 = ussm.astype(BF16).reshape(INPROJ_TILE // SSM_CHUNK, SSM_CHUNK, SSM_WIDTH)
    for t in range(SSM_CHUNK):
        ut = u3[:, t, :]
        for g in range(SSM_GROUPS):
            upack_ref[g, :, t * SSM_GROUP_DIM:(t + 1) * SSM_GROUP_DIM] = (
                ut[:, g * SSM_GROUP_DIM:(g + 1) * SSM_GROUP_DIM])
    gate_lo = FN_WIDTH + 3 * NA_WIDTH + SSM_WIDTH
    for j in range(N_BRANCH):
        gate_ref[:, j * D_MODEL:(j + 1) * D_MODEL] = jax.nn.sigmoid(seg(gate_lo + j * D_MODEL, D_MODEL)).astype(BF16)


def _inproj(x, g_mix, w_in, layer):
    n = x.shape[0]
    tm = INPROJ_TILE
    row = lambda width: pl.BlockSpec((tm, width), lambda i: (i, 0))
    return pl.pallas_call(
        _inproj_kernel,
        grid=(n // tm,),
        in_specs=[row(D_MODEL),
                  _resident((None, 1, D_MODEL), lambda i: (layer, 0, 0)),
                  _resident((None, D_MODEL, D_IN), lambda i: (layer, 0, 0))],
        out_specs=[row(FN_WIDTH), row(NA_WIDTH), row(NA_WIDTH), row(NA_WIDTH), row(SSM_WIDTH),
                   pl.BlockSpec((SSM_GROUPS, tm // SSM_CHUNK, SSM_CK), lambda i: (0, i, 0)),
                   row(N_BRANCH * D_MODEL)],
        out_shape=[jax.ShapeDtypeStruct((n, FN_WIDTH), F32),
                   jax.ShapeDtypeStruct((n, NA_WIDTH), BF16),
                   jax.ShapeDtypeStruct((n, NA_WIDTH), BF16),
                   jax.ShapeDtypeStruct((n, NA_WIDTH), BF16),
                   jax.ShapeDtypeStruct((n, SSM_WIDTH), F32),
                   jax.ShapeDtypeStruct((SSM_GROUPS, n // SSM_CHUNK, SSM_CK), BF16),
                   jax.ShapeDtypeStruct((n, N_BRANCH * D_MODEL), BF16)],
        compiler_params=_cparams(1),
        name="inproj",
    )(x, g_mix, w_in)


def _dft_tables(seq):
    two_pi = 2.0 * math.pi
    shape = (2 * DFT_N1 * DFT_A_N2, DFT_N1 * DFT_A_N2)
    row = lax.broadcasted_iota(jnp.int32, shape, 0)
    col = lax.broadcasted_iota(jnp.int32, shape, 1)
    k1, n1 = (row // DFT_A_N2) % DFT_N1, col // DFT_A_N2
    ang = ((k1 * n1) % DFT_N1).astype(F32) * (two_pi / DFT_N1)
    wa = jnp.where(row < DFT_N1 * DFT_A_N2, jnp.cos(ang), -jnp.sin(ang))
    wa = jnp.where(row % DFT_A_N2 == col % DFT_A_N2, wa, 0.0)

    shape = (DFT_N1, DFT_N2, DFT_N2)
    kk1 = lax.broadcasted_iota(jnp.int32, shape, 0)
    kk2 = lax.broadcasted_iota(jnp.int32, shape, 1)
    nn2 = lax.broadcasted_iota(jnp.int32, shape, 2)
    ang = ((nn2 * (kk1 + DFT_N1 * kk2)) % seq).astype(F32) * (two_pi / seq)
    c, s = jnp.cos(ang), jnp.sin(ang)
    g = jnp.concatenate([jnp.concatenate([c, s], axis=2), jnp.concatenate([-s, c], axis=2)], axis=1)

    m = lax.broadcasted_iota(jnp.int32, (FN_WIDTH, FN_WIDTH), 0)
    j = lax.broadcasted_iota(jnp.int32, (FN_WIDTH, FN_WIDTH), 1)
    same_group = (m // FN_GROUP_DIM) == (j // FN_GROUP_DIM)
    ang = ((m * j) % FN_GROUP_DIM).astype(F32) * (two_pi / FN_GROUP_DIM)
    scale = 1.0 / math.sqrt(seq * FN_GROUP_DIM)
    cc = jnp.where(same_group, jnp.cos(ang) * scale, 0.0)
    sc = jnp.where(same_group, jnp.sin(ang) * scale, 0.0)
    return wa.astype(BF16), g.astype(BF16), cc.astype(BF16), sc.astype(BF16)


DFT_A_N2 = 8
DFT_C_K1 = 16


def _dft_a_kernel(w_ref, x_ref, o_ref):
    x = x_ref[...].reshape(DFT_N1 * DFT_A_N2, FN_WIDTH).astype(BF16)
    a = jnp.dot(w_ref[...], x, preferred_element_type=F32)
    o_ref[...] = a.reshape(2, DFT_N1, DFT_A_N2, FN_WIDTH)


def _dft_c_kernel(ar_ref, ai_ref, g_ref, cc_ref, sc_ref, o_ref, zr_ref, zi_ref):
    for j in range(DFT_C_K1):
        a = jnp.concatenate([ar_ref[j], ai_ref[j]], axis=0).astype(BF16)
        z = jnp.dot(g_ref[j], a, preferred_element_type=F32)
        zr_ref[j * DFT_N2:(j + 1) * DFT_N2, :] = z[:DFT_N2].astype(BF16)
        zi_ref[j * DFT_N2:(j + 1) * DFT_N2, :] = z[DFT_N2:].astype(BF16)
    o = (jnp.dot(zr_ref[...], cc_ref[...], preferred_element_type=F32)
         + jnp.dot(zi_ref[...], sc_ref[...], preferred_element_type=F32)).astype(BF16)
    for j in range(DFT_C_K1):
        o_ref[:, j, :] = o[j * DFT_N2:(j + 1) * DFT_N2]


def _fourier_mix(ufn, batch, seq, tables):
    wa, g, cc, sc = tables
    x = ufn.reshape(batch, DFT_N1, DFT_N2, FN_WIDTH)
    nb = DFT_A_N2
    a = pl.pallas_call(
        _dft_a_kernel,
        grid=(batch, DFT_N2 // nb),
        in_specs=[_resident((2 * DFT_N1 * nb, DFT_N1 * nb), lambda b, j: (0, 0)),
                  pl.BlockSpec((None, DFT_N1, nb, FN_WIDTH), lambda b, j: (b, 0, j, 0))],
        out_specs=pl.BlockSpec((None, 2, DFT_N1, nb, FN_WIDTH), lambda b, j: (b, 0, 0, j, 0)),
        out_shape=jax.ShapeDtypeStruct((batch, 2, DFT_N1, DFT_N2, FN_WIDTH), F32),
        compiler_params=_cparams(2),
        name="dft_a",
    )(wa, x)
    kb = DFT_C_K1
    part = lambda p: pl.BlockSpec((None, None, kb, DFT_N2, FN_WIDTH), lambda b, i: (b, p, i, 0, 0))
    out = pl.pallas_call(
        _dft_c_kernel,
        grid=(batch, DFT_N1 // kb),
        in_specs=[part(0), part(1),
                  pl.BlockSpec((kb, 2 * DFT_N2, 2 * DFT_N2), lambda b, i: (i, 0, 0)),
                  _resident((FN_WIDTH, FN_WIDTH), lambda b, i: (0, 0)),
                  _resident((FN_WIDTH, FN_WIDTH), lambda b, i: (0, 0))],
        out_specs=pl.BlockSpec((None, DFT_N2, kb, FN_WIDTH), lambda b, i: (b, 0, i, 0)),
        out_shape=jax.ShapeDtypeStruct((batch, DFT_N2, DFT_N1, FN_WIDTH), BF16),
        scratch_shapes=[pltpu.VMEM((kb * DFT_N2, FN_WIDTH), BF16), pltpu.VMEM((kb * DFT_N2, FN_WIDTH), BF16)],
        compiler_params=_cparams(2),
        name="dft_c",
    )(a, a, g, cc, sc)
    return out.reshape(batch * seq, FN_WIDTH)


NA_DR = 2 * NA_KR - 1


def _na_bias_slabs(rpb):
    depth = rpb.shape[0]
    pad = GRID_W - NA_KC
    period = 2 * GRID_W - 1
    ext = jnp.pad(rpb * LOG2_E, ((0, 0), (0, 0), (0, 0), (pad, pad)))
    flat = jnp.tile(ext, (1, 1, 1, GRID_W))[..., GRID_W - 1:GRID_W - 1 + GRID_W * (period - 1)]
    slab = flat.reshape(depth, NA_HEADS, NA_DR, GRID_W, period - 1)[..., :GRID_W]
    qc, kc = np.arange(GRID_W)[:, None], np.arange(GRID_W)[None, :]
    col_start = np.clip(qc - NA_KC // 2, 0, GRID_W - NA_KC)
    col_ok = (kc >= col_start) & (kc < col_start + NA_KC)
    slab = jnp.where(col_ok, slab, NEG_INF)
    return jnp.concatenate([slab[:, :, :-1], slab[:, :, 1:]], axis=-1)


def _na_row_masks(rows):
    kr = min(NA_KR, rows)
    n_blocks = rows // NA_RQ
    out = np.zeros((3, 2, 3, NA_QB, 2 * NA_HEAD_DIM), np.float32)
    for v, blk in enumerate((0, 1, n_blocks - 1)):
        r = blk * NA_RQ + np.arange(NA_RQ)[:, None]
        key_row = (blk - 1) * NA_RQ + np.arange(3 * NA_RQ)[None, :]
        row_start = np.clip(r - kr // 2, 0, rows - kr)
        row_ok = (key_row >= row_start) & (key_row < row_start + kr)
        pen = np.where(row_ok, 0.0, NEG_INF).reshape(NA_RQ, 3, NA_RQ)
        pen = np.repeat(pen.transpose(1, 2, 0), GRID_W, axis=1)
        for e in range(2):
            lo = NA_HEAD_DIM * (1 - e)
            out[v, e, :, :, lo:lo + NA_RQ] = pen
    return out


def _na_kernel(q_ref, kp_ref, kc_ref, kn_ref, vp_ref, vc_ref, vn_ref, slab_ref, rmask_ref, o_ref):
    lane = lax.broadcasted_iota(jnp.int32, (1, 2 * NA_HEAD_DIM), 1)
    first = lane < NA_HEAD_DIM
    q_row = lax.broadcasted_iota(jnp.int32, (NA_QB, 2 * NA_HEAD_DIM), 0) // GRID_W
    q_lane = lax.broadcasted_iota(jnp.int32, (NA_QB, 2 * NA_HEAD_DIM), 1)
    k_refs = (kp_ref, kc_ref, kn_ref)
    v_refs = (vp_ref, vc_ref, vn_ref)
    for hp in range(NA_HEADS // 2):
        sl = slice(2 * NA_HEAD_DIM * hp, 2 * NA_HEAD_DIM * (hp + 1))
        q2 = q_ref[:, sl]
        acc = []
        for e in range(2):
            mine = first if e == 0 else jnp.logical_not(first)
            row_onehot = (q_lane == NA_HEAD_DIM * (1 - e) + q_row).astype(BF16)
            qm = jnp.where(mine, q2, row_onehot)
            ms, rs = [], []
            for i in range(3):
                km = jnp.where(mine, k_refs[i][:, sl], rmask_ref[e, i])
                s = lax.dot_general(qm, km, (((1,), (1,)), ((), ())), preferred_element_type=F32)
                bias = jnp.concatenate(
                    [jnp.concatenate([slab_ref[2 * hp + e, NA_RQ * i + 2 * m - a + NA_KR - 1 - NA_RQ]
                                      for m in range(NA_RQ // 2)], axis=1) for a in range(NA_RQ)], axis=0)
                s = s + bias
                m = jnp.max(s, axis=-1, keepdims=True)
                p = jnp.exp2(s - m).astype(BF16)
                v2 = v_refs[i][:, sl]
                vm = jnp.where(mine, v2, jnp.ones_like(v2))
                ms.append(m)
                rs.append(jnp.dot(p, vm, preferred_element_type=F32))
            m_all = jnp.maximum(jnp.maximum(ms[0], ms[1]), ms[2])
            acc.append(rs[0] * jnp.exp2(ms[0] - m_all) + rs[1] * jnp.exp2(ms[1] - m_all)
                       + rs[2] * jnp.exp2(ms[2] - m_all))
        num = jnp.where(first, acc[0], acc[1])
        den = pltpu.roll(jnp.where(first, acc[1], acc[0]), NA_HEAD_DIM, 1)
        o_ref[:, sl] = (num / den).astype(BF16)


def _neighbourhood_attention(q, k, v, slabs, row_masks, layer, batch, seq):
    n_blocks = seq // NA_QB
    q3, k3, v3 = (t.reshape(batch, seq, NA_WIDTH) for t in (q, k, v))
    blk = lambda shift: pl.BlockSpec(
        (None, NA_QB, NA_WIDTH), lambda b, i: (b, jnp.clip(i + shift, 0, n_blocks - 1), 0))
    variant = lambda b, i: (jnp.where(i == 0, 0, jnp.where(i == n_blocks - 1, 2, 1)), 0, 0, 0, 0)
    out = pl.pallas_call(
        _na_kernel,
        grid=(batch, n_blocks),
        in_specs=[blk(0), blk(-1), blk(0), blk(1), blk(-1), blk(0), blk(1),
                  _resident((None, NA_HEADS, NA_DR - 1, GRID_W, 2 * GRID_W), lambda b, i: (layer, 0, 0, 0, 0)),
                  pl.BlockSpec((None, 2, 3, NA_QB, 2 * NA_HEAD_DIM), variant)],
        out_specs=blk(0),
        out_shape=jax.ShapeDtypeStruct((batch, seq, NA_WIDTH), BF16),
        compiler_params=_cparams(2),
        name="natten",
    )(q3, k3, k3, k3, v3, v3, v3, slabs, row_masks)
    return out.reshape(batch * seq, NA_WIDTH)


def _cmul(ar, ai, br, bi):
    return ar * br - ai * bi, ar * bi + ai * br


def _ssm_table_kernel(pr_ref, pi_ref, bbr_ref, bbi_ref, cr_ref, ci_ref, intra_ref, sp_ref, qt_ref):
    n, c = SSM_CHUNK, SSM_GROUP_DIM
    hi = lax.Precision.HIGHEST
    lanes_contract = (((1,), (1,)), ((), ()))
    sp_ref[...] = jnp.zeros(sp_ref.shape, sp_ref.dtype)
    qt_ref[...] = jnp.zeros(qt_ref.shape, qt_ref.dtype)
    spread = (lax.broadcasted_iota(jnp.int32, (c, SSM_CK), 1) % c
              == lax.broadcasted_iota(jnp.int32, (c, SSM_CK), 0)).astype(F32)
    delta = (lax.broadcasted_iota(jnp.int32, (SSM_CK, SSM_CK), 1) // c
             - lax.broadcasted_iota(jnp.int32, (SSM_CK, SSM_CK), 0) // c)
    for e in range(2):
        toeplitz = []
        for d in range(2):
            pr, pi = pr_ref[d, e], pi_ref[d, e]
            w = [_cmul(bbr_ref[d, e], bbi_ref[d, e], pr[t:t + 1], pi[t:t + 1]) for t in range(n)]
            wr = jnp.concatenate([x[0] for x in w], axis=0)
            wi = jnp.concatenate([x[1] for x in w], axis=0)
            kt = (lax.dot_general(wr, cr_ref[d, e], lanes_contract, precision=hi, preferred_element_type=F32)
                  - lax.dot_general(wi, ci_ref[d, e], lanes_contract, precision=hi, preferred_element_type=F32))
            toeplitz.append(jnp.dot(kt, spread, precision=hi, preferred_element_type=F32))
            order = range(n - 1, -1, -1) if d == 0 else range(n)
            for ri in range(2):
                col = (2 * d + ri) * 2 * SSM_STATE + e * SSM_STATE
                sp_ref[e, :, col:col + SSM_STATE] = jnp.concatenate([w[t][ri] for t in order], axis=0).astype(BF16)
            exps = range(1, n + 1) if d == 0 else range(n, 0, -1)
            q = [_cmul(cr_ref[d, e], ci_ref[d, e], pr[m:m + 1], pi[m:m + 1]) for m in exps]
            rows = slice(e * SSM_CK, (e + 1) * SSM_CK)
            cols = slice(e * SSM_STATE, (e + 1) * SSM_STATE)
            qt_ref[2 * d, rows, cols] = jnp.concatenate([x[0] for x in q], axis=0).astype(BF16)
            qt_ref[2 * d + 1, rows, cols] = (-jnp.concatenate([x[1] for x in q], axis=0)).astype(BF16)
        fwd, bwd = toeplitz
        acc = jnp.zeros((SSM_CK, SSM_CK), F32)
        for dlt in range(-(n - 1), n):
            if dlt > 0:
                src = fwd[dlt * c:(dlt + 1) * c]
            elif dlt < 0:
                src = bwd[-dlt * c:(-dlt + 1) * c]
            else:
                src = fwd[:c] + bwd[:c]
            acc = jnp.where(delta == dlt, jnp.concatenate([src] * n, axis=0), acc)
        intra_ref[e] = acc.astype(BF16)


def _ssm_tables(a_re, a_im, log_dt, b_re, b_im, c_re, c_im):
    depth = a_re.shape[0]
    dt = jnp.exp(log_dt)[..., None]
    zr, zi = a_re * dt, a_im * dt

    def powers(taus):
        taus = taus.astype(F32)[:, None]
        mag = jnp.exp(zr[..., None, :] * taus)
        return mag * jnp.cos(zi[..., None, :] * taus), mag * jnp.sin(zi[..., None, :] * taus)

    pr, pi = powers(jnp.arange(SSM_CHUNK + 1))
    lbr, lbi = pr[..., 1, :], pi[..., 1, :]
    den = a_re * a_re + a_im * a_im
    fr = ((lbr - 1.0) * a_re + lbi * a_im) / den
    fi = (lbi * a_re - (lbr - 1.0) * a_im) / den
    bbr, bbi = _cmul(fr[..., None, :], fi[..., None, :],
                     jnp.swapaxes(b_re, -1, -2), jnp.swapaxes(b_im, -1, -2))

    pair = lambda rows: pl.BlockSpec((None, 2, 2, rows, SSM_STATE), lambda l, p: (l, 0, p, 0, 0))
    intra, sproj, qproj_t = pl.pallas_call(
        _ssm_table_kernel,
        grid=(depth, SSM_PAIRS),
        in_specs=[pair(SSM_CHUNK + 1), pair(SSM_CHUNK + 1), pair(SSM_GROUP_DIM), pair(SSM_GROUP_DIM),
                  pair(SSM_GROUP_DIM), pair(SSM_GROUP_DIM)],
        out_specs=[pl.BlockSpec((None, 2, SSM_CK, SSM_CK), lambda l, p: (l, p, 0, 0)),
                   pl.BlockSpec((None, None, 2, SSM_CK, 8 * SSM_STATE), lambda l, p: (l, p, 0, 0, 0)),
                   pl.BlockSpec((None, None, 4, 2 * SSM_CK, 2 * SSM_STATE), lambda l, p: (l, p, 0, 0, 0))],
        out_shape=[jax.ShapeDtypeStruct((depth, SSM_GROUPS, SSM_CK, SSM_CK), BF16),
                   jax.ShapeDtypeStruct((depth, SSM_PAIRS, 2, SSM_CK, 8 * SSM_STATE), BF16),
                   jax.ShapeDtypeStruct((depth, SSM_PAIRS, 4, 2 * SSM_CK, 2 * SSM_STATE), BF16)],
        compiler_params=_cparams(2),
        name="ssm_tables",
    )(pr, pi, bbr, bbi, c_re, c_im)

    sr, si = powers(jnp.arange(1, 9) * SSM_CHUNK)
    sr, si = sr.transpose(0, 1, 3, 2, 4), si.transpose(0, 1, 3, 2, 4)
    lam_pows = jnp.stack([jnp.stack([sr[:, 0], si[:, 0]], axis=1),
                          jnp.stack([sr[:, 1, ::-1], si[:, 1, ::-1]], axis=1)], axis=1)
    lam_pows = lam_pows.reshape(depth, 4 * 8, SSM_GROUPS * SSM_STATE)
    return intra, sproj, qproj_t, lam_pows


SSM_ROW_TILE = 1024


def _ssm_chunk_states(u_ref, sp_ref, st_ref):
    for r in range(u_ref.shape[1] // SSM_ROW_TILE):
        rows = slice(r * SSM_ROW_TILE, (r + 1) * SSM_ROW_TILE)
        st = (jnp.dot(u_ref[0, rows, :], sp_ref[0], preferred_element_type=F32)
              + jnp.dot(u_ref[1, rows, :], sp_ref[1], preferred_element_type=F32))
        for part in range(4):
            st_ref[part, rows, :] = st[:, 128 * part:128 * (part + 1)]


def _ssm_carry_states(st_ref, lam_ref, xin_ref, n_seq):
    n_chunks, lanes = st_ref.shape[1] // n_seq, st_ref.shape[2]
    n_tiles = n_chunks // 8
    row = lax.broadcasted_iota(jnp.int32, (8, lanes), 0)

    def bcast(x, r):
        return jnp.broadcast_to(x[r:r + 1, :], (8, lanes))

    tabs = []
    for d in range(2):
        pr, pi = lam_ref[16 * d:16 * d + 8, :], lam_ref[16 * d + 8:16 * d + 16, :]
        at = (lambda m: m - 1) if d == 0 else (lambda m: 8 - m)
        tabs.append((pr, pi, [(bcast(pr, at(m)), bcast(pi, at(m))) for m in (1, 2, 4)]))

    def shifted(x, dist, forward):
        if forward:
            return jnp.where(row >= dist, pltpu.roll(x, dist, 0), 0.0)
        return jnp.where(row < 8 - dist, pltpu.roll(x, 8 - dist, 0), 0.0)

    def scan_tile(sr, si, cr, ci, d):
        forward = d == 0
        pr, pi, steps = tabs[d]
        hr, hi = sr, si
        for (lr, li), dist in zip(steps, (1, 2, 4)):
            mr, mi = _cmul(lr, li, shifted(hr, dist, forward), shifted(hi, dist, forward))
            hr, hi = hr + mr, hi + mi
        mr, mi = _cmul(pr, pi, cr, ci)
        er, ei = hr + mr, hi + mi
        if forward:
            xr = jnp.where(row >= 1, pltpu.roll(er, 1, 0), cr)
            xi = jnp.where(row >= 1, pltpu.roll(ei, 1, 0), ci)
            return xr, xi, bcast(er, 7), bcast(ei, 7)
        xr = jnp.where(row < 7, pltpu.roll(er, 7, 0), cr)
        xi = jnp.where(row < 7, pltpu.roll(ei, 7, 0), ci)
        return xr, xi, bcast(er, 0), bcast(ei, 0)

    def body(i, carry):
        new = []
        for s in range(n_seq):
            for d in range(2):
                cr, ci = carry[2 * (2 * s + d)], carry[2 * (2 * s + d) + 1]
                r0 = pl.multiple_of(s * n_chunks + (i if d == 0 else n_tiles - 1 - i) * 8, 8)
                sr, si = st_ref[2 * d, pl.ds(r0, 8), :], st_ref[2 * d + 1, pl.ds(r0, 8), :]
                xr, xi, ncr, nci = scan_tile(sr, si, cr, ci, d)
                xin_ref[2 * d, pl.ds(r0, 8), :] = xr
                xin_ref[2 * d + 1, pl.ds(r0, 8), :] = xi
                new += [ncr, nci]
        return tuple(new)

    zero = jnp.zeros((8, lanes), F32)
    lax.fori_loop(0, n_tiles, body, (zero,) * (4 * n_seq))


def _ssm_outputs(u_ref, xin_ref, mi_ref, qt_ref, o_ref):
    lanes_contract = (((1,), (1,)), ((), ()))
    for r in range(u_ref.shape[1] // SSM_ROW_TILE):
        rows = slice(r * SSM_ROW_TILE, (r + 1) * SSM_ROW_TILE)
        acc = lax.dot_general(xin_ref[0, rows, :].astype(BF16), qt_ref[0], lanes_contract,
                              preferred_element_type=F32)
        for part in range(1, 4):
            acc = acc + lax.dot_general(xin_ref[part, rows, :].astype(BF16), qt_ref[part], lanes_contract,
                                        preferred_element_type=F32)
        for e in range(2):
            y0 = jnp.dot(u_ref[e, rows, :], mi_ref[e], preferred_element_type=F32)
            o_ref[e, rows, :] = y0 + acc[:, SSM_CK * e:SSM_CK * (e + 1)]


def _ssm_pair_kernel(u_ref, sp_ref, lam_ref, mi_ref, qt_ref, o_ref, st_ref, xin_ref, *, n_seq):
    _ssm_chunk_states(u_ref, sp_ref, st_ref)
    _ssm_carry_states(st_ref, lam_ref, xin_ref, n_seq)
    _ssm_outputs(u_ref, xin_ref, mi_ref, qt_ref, o_ref)


def _ssm_scan(upack, batch, seq, tables, layer):
    intra, sproj, qproj_t, lam_pows = tables
    rows = (seq // SSM_CHUNK) * batch
    lanes = 2 * SSM_STATE
    pair_rows = pl.BlockSpec((2, rows, SSM_CK), lambda p: (p, 0, 0))
    return pl.pallas_call(
        functools.partial(_ssm_pair_kernel, n_seq=batch),
        grid=(SSM_PAIRS,),
        in_specs=[pair_rows,
                  pl.BlockSpec((None, None, 2, SSM_CK, 4 * lanes), lambda p: (layer, p, 0, 0, 0)),
                  pl.BlockSpec((None, 32, lanes), lambda p: (layer, 0, p)),
                  pl.BlockSpec((None, 2, SSM_CK, SSM_CK), lambda p: (layer, p, 0, 0)),
                  pl.BlockSpec((None, None, 4, 2 * SSM_CK, lanes), lambda p: (layer, p, 0, 0, 0))],
        out_specs=pair_rows,
        out_shape=jax.ShapeDtypeStruct((SSM_GROUPS, rows, SSM_CK), F32),
        scratch_shapes=[pltpu.VMEM((4, rows, lanes), F32), pltpu.VMEM((4, rows, lanes), F32)],
        compiler_params=_cparams(1),
        name="ssm_pair",
    )(upack, sproj, lam_pows, intra, qproj_t)


def _merge_tile(x_ref, gate_ref, fm_ref, na_ref, ypack_ref, ussm_ref, dskip_ref, wglu_ref,
                wfn_ref, wna_ref, wssm_ref, wout_ref, ytok_ref, yrow_ref):
    y_fn = jnp.dot(fm_ref[...], wfn_ref[...], preferred_element_type=F32)
    y_na = jnp.dot(na_ref[...], wna_ref[...], preferred_element_type=F32)
    merged = (gate_ref[:, :D_MODEL].astype(F32) * y_fn
              + gate_ref[:, D_MODEL:2 * D_MODEL].astype(F32) * y_na)
    for t in range(SSM_CHUNK):
        for g in range(SSM_GROUPS):
            yrow_ref[:, g * SSM_GROUP_DIM:(g + 1) * SSM_GROUP_DIM] = (
                ypack_ref[g, :, t * SSM_GROUP_DIM:(t + 1) * SSM_GROUP_DIM])
        ytok_ref[:, t, :] = yrow_ref[...]
    yscan = ytok_ref[...].reshape(TOKEN_TILE, SSM_WIDTH)
    y = yscan + dskip_ref[...] * ussm_ref[...]
    y = jax.nn.gelu(y)
    y = y * jax.nn.sigmoid(jnp.dot(y.astype(BF16), wglu_ref[...], preferred_element_type=F32))
    y_ssm = jnp.dot(y.astype(BF16), wssm_ref[...], preferred_element_type=F32)
    merged = merged + gate_ref[:, 2 * D_MODEL:].astype(F32) * y_ssm
    return x_ref[...] + jnp.dot(merged.astype(BF16), wout_ref[...], preferred_element_type=F32)


def _ffn_tile(x, g_ref, wup_ref, wdown_ref, gfin_ref, up_ref, final_norm):
    h = _rms(x, g_ref[...]).astype(BF16)
    for c in range(D_FF // D_MODEL):
        cols = slice(c * D_MODEL, (c + 1) * D_MODEL)
        a = jnp.maximum(jnp.dot(h, wup_ref[:, cols], preferred_element_type=F32), 0.0)
        up_ref[:, cols] = (a * a).astype(BF16)
    y = x + jnp.dot(up_ref[...], wdown_ref[...], preferred_element_type=F32)
    if final_norm:
        y = _rms(y, gfin_ref[...])
    return y


def _mix_ffn_kernel(x_ref, gate_ref, fm_ref, na_ref, ypack_ref, ussm_ref, dskip_ref, wglu_ref,
                    wfn_ref, wna_ref, wssm_ref, wout_ref, g_ref, wup_ref, wdown_ref, gfin_ref,
                    o_ref, ytok_ref, yrow_ref, up_ref, *, final_norm):
    x1 = _merge_tile(x_ref, gate_ref, fm_ref, na_ref, ypack_ref, ussm_ref, dskip_ref, wglu_ref,
                     wfn_ref, wna_ref, wssm_ref, wout_ref, ytok_ref, yrow_ref)
    o_ref[...] = _ffn_tile(x1, g_ref, wup_ref, wdown_ref, gfin_ref, up_ref, final_norm)


def _mix_ffn(x, gates, fm, na, ypack, ussm, ssm_d, w_glu, w_br_fn, w_br_na, w_br_ssm, w_out,
             g_ffn, w_up, w_down, g_final, layer, final_norm):
    n = x.shape[0]
    tm = TOKEN_TILE
    row = lambda width: pl.BlockSpec((tm, width), lambda i: (i, 0))
    wspec = lambda k, m: _resident((None, k, m), lambda i: (layer, 0, 0))
    return pl.pallas_call(
        functools.partial(_mix_ffn_kernel, final_norm=final_norm),
        grid=(n // tm,),
        in_specs=[row(D_MODEL), row(N_BRANCH * D_MODEL), row(FN_WIDTH), row(NA_WIDTH),
                  pl.BlockSpec((SSM_GROUPS, tm // SSM_CHUNK, SSM_CK), lambda i: (0, i, 0)),
                  row(SSM_WIDTH), wspec(1, SSM_WIDTH), wspec(SSM_WIDTH, SSM_WIDTH),
                  wspec(FN_WIDTH, D_MODEL), wspec(NA_WIDTH, D_MODEL), wspec(SSM_WIDTH, D_MODEL),
                  wspec(D_MODEL, D_MODEL),
                  wspec(1, D_MODEL), wspec(D_MODEL, D_FF), wspec(D_FF, D_MODEL),
                  _resident((1, D_MODEL), lambda i: (0, 0))],
        out_specs=row(D_MODEL),
        out_shape=jax.ShapeDtypeStruct((n, D_MODEL), F32),
        scratch_shapes=[pltpu.VMEM((tm // SSM_CHUNK, SSM_CHUNK, SSM_WIDTH), F32),
                        pltpu.VMEM((tm // SSM_CHUNK, SSM_WIDTH), F32),
                        pltpu.VMEM((tm, D_FF), BF16)],
        compiler_params=_cparams(1),
        name="mix_ffn",
    )(x, gates, fm, na, ypack, ussm, ssm_d, w_glu, w_br_fn, w_br_na, w_br_ssm, w_out,
      g_ffn, w_up, w_down, g_final)


def kernel(x, g_mix, w_in, na_rpb, ssm_a_re, ssm_a_im, ssm_log_dt, ssm_b_re, ssm_b_im, ssm_c_re, ssm_c_im,
           ssm_d, w_glu, w_br_fn, w_br_na, w_br_ssm, w_out, g_ffn, w_up, w_down, g_final):
    batch, seq, _ = x.shape
    depth = w_in.shape[0]
    assert seq == DFT_N1 * DFT_N2 and seq % (GRID_W * NA_RQ) == 0 and (batch * seq) % TOKEN_TILE == 0
    rows = seq // GRID_W

    to_bf16 = lambda w: w.astype(BF16)
    w_in, w_glu, w_br_fn, w_br_na, w_br_ssm, w_out, w_up, w_down = map(
        to_bf16, (w_in, w_glu, w_br_fn, w_br_na, w_br_ssm, w_out, w_up, w_down))
    g_mix3 = g_mix.reshape(depth, 1, D_MODEL)
    g_ffn3 = g_ffn.reshape(depth, 1, D_MODEL)
    ssm_d3 = ssm_d.reshape(depth, 1, SSM_WIDTH)
    g_final2 = g_final.reshape(1, D_MODEL)
    dft_tables = _dft_tables(seq)
    na_slabs = _na_bias_slabs(na_rpb)
    na_row_masks = jnp.asarray(_na_row_masks(rows)).astype(BF16)
    ssm_tables = _ssm_tables(ssm_a_re, ssm_a_im, ssm_log_dt, ssm_b_re, ssm_b_im, ssm_c_re, ssm_c_im)

    xs = x.reshape(batch * seq, D_MODEL)
    for l in range(depth):
        ufn, q, k, v, ussm, upack, gates = _inproj(xs, g_mix3, w_in, l)
        fm = _fourier_mix(ufn, batch, seq, dft_tables)
        na = _neighbourhood_attention(q, k, v, na_slabs, na_row_masks, l, batch, seq)
        ypack = _ssm_scan(upack, batch, seq, ssm_tables, l)
        xs = _mix_ffn(xs, gates, fm, na, ypack, ussm, ssm_d3, w_glu, w_br_fn, w_br_na, w_br_ssm, w_out,
                      g_ffn3, w_up, w_down, g_final2, l, final_norm=(l == depth - 1))
    return xs.reshape(batch, seq, D_MODEL)
```

```python
import functools
import math

import numpy as np
import jax
import jax.numpy as jnp
from jax import lax
from jax.experimental import pallas as pl
from jax.experimental.pallas import tpu as pltpu

F32 = jnp.float32
BF16 = jnp.bfloat16

D_MODEL = 1024
GRID_W = 64
FN_GROUP_DIM = 64
FN_WIDTH = 256
NA_HEADS = 8
NA_HEAD_DIM = 64
NA_WIDTH = 512
NA_KR = 8
NA_KC = 16
NEG_INF = -1e30
SSM_GROUPS = 16
SSM_GROUP_DIM = 16
SSM_WIDTH = 256
SSM_STATE = 64
N_BRANCH = 3
D_FF = 4 * D_MODEL
D_IN = FN_WIDTH + 3 * NA_WIDTH + SSM_WIDTH + N_BRANCH * D_MODEL
RMS_EPS = 1e-6
LOG2_E = 1.0 / math.log(2.0)

DFT_N1 = 128
DFT_N2 = 64
NA_RQ = 4
NA_QB = NA_RQ * GRID_W
SSM_CHUNK = 16
SSM_PAIRS = SSM_GROUPS // 2
SSM_CK = SSM_CHUNK * SSM_GROUP_DIM

TOKEN_TILE = 512
INPROJ_TILE = 1024
VMEM_LIMIT = 56 * 1024 * 1024


def _cparams(n_axes):
    return pltpu.CompilerParams(dimension_semantics=("arbitrary",) * n_axes, vmem_limit_bytes=VMEM_LIMIT)


def _resident(block_shape, index_map):
    return pl.BlockSpec(block_shape, index_map, pipeline_mode=pl.Buffered(1))


def _rms(x, g):
    ms = jnp.mean(x * x, axis=-1, keepdims=True)
    return x * lax.rsqrt(ms + RMS_EPS) * g


def _inproj_kernel(x_ref, g_ref, w_ref, ufn_ref, q_ref, k_ref, v_ref, ussm_ref, upack_ref, gate_ref):
    h = _rms(x_ref[...], g_ref[...]).astype(BF16)

    def seg(lo, width):
        return jnp.dot(h, w_ref[:, lo:lo + width], preferred_element_type=F32)

    ufn_ref[...] = seg(0, FN_WIDTH)
    q_ref[...] = (seg(FN_WIDTH, NA_WIDTH) * (NA_HEAD_DIM ** -0.5 * LOG2_E)).astype(BF16)
    k_ref[...] = seg(FN_WIDTH + NA_WIDTH, NA_WIDTH).astype(BF16)
    v_ref[...] = seg(FN_WIDTH + 2 * NA_WIDTH, NA_WIDTH).astype(BF16)
    ussm = seg(FN_WIDTH + 3 * NA_WIDTH, SSM_WIDTH)
    ussm_ref[...] = ussm
    u3 = ussm.astype(BF16).reshape(INPROJ_TILE // SSM_CHUNK, SSM_CHUNK, SSM_WIDTH)
    for t in range(SSM_CHUNK):
        ut = u3[:, t, :]
        for g in range(SSM_GROUPS):
            upack_ref[g, :, t * SSM_GROUP_DIM:(t + 1) * SSM_GROUP_DIM] = (
                ut[:, g * SSM_GROUP_DIM:(g + 1) * SSM_GROUP_DIM])
    gate_lo = FN_WIDTH + 3 * NA_WIDTH + SSM_WIDTH
    for j in range(N_BRANCH):
        gate_ref[:, j * D_MODEL:(j + 1) * D_MODEL] = jax.nn.sigmoid(seg(gate_lo + j * D_MODEL, D_MODEL)).astype(BF16)


def _inproj(x, g_mix, w_in, layer):
    n = x.shape[0]
    tm = INPROJ_TILE
    row = lambda width: pl.BlockSpec((tm, width), lambda i: (i, 0))
    return pl.pallas_call(
        _inproj_kernel,
        grid=(n // tm,),
        in_specs=[row(D_MODEL),
                  _resident((None, 1, D_MODEL), lambda i: (layer, 0, 0)),
                  _resident((None, D_MODEL, D_IN), lambda i: (layer, 0, 0))],
        out_specs=[row(FN_WIDTH), row(NA_WIDTH), row(NA_WIDTH), row(NA_WIDTH), row(SSM_WIDTH),
                   pl.BlockSpec((SSM_GROUPS, tm // SSM_CHUNK, SSM_CK), lambda i: (0, i, 0)),
                   row(N_BRANCH * D_MODEL)],
        out_shape=[jax.ShapeDtypeStruct((n, FN_WIDTH), F32),
                   jax.ShapeDtypeStruct((n, NA_WIDTH), BF16),
                   jax.ShapeDtypeStruct((n, NA_WIDTH), BF16),
                   jax.ShapeDtypeStruct((n, NA_WIDTH), BF16),
                   jax.ShapeDtypeStruct((n, SSM_WIDTH), F32),
                   jax.ShapeDtypeStruct((SSM_GROUPS, n // SSM_CHUNK, SSM_CK), BF16),
                   jax.ShapeDtypeStruct((n, N_BRANCH * D_MODEL), BF16)],
        compiler_params=_cparams(1),
        name="inproj",
    )(x, g_mix, w_in)


def _dft_tables(seq):
    two_pi = 2.0 * math.pi
    shape = (2 * DFT_N1 * DFT_A_N2, DFT_N1 * DFT_A_N2)
    row = lax.broadcasted_iota(jnp.int32, shape, 0)
    col = lax.broadcasted_iota(jnp.int32, shape, 1)
    k1, n1 = (row // DFT_A_N2) % DFT_N1, col // DFT_A_N2
    ang = ((k1 * n1) % DFT_N1).astype(F32) * (two_pi / DFT_N1)
    wa = jnp.where(row < DFT_N1 * DFT_A_N2, jnp.cos(ang), -jnp.sin(ang))
    wa = jnp.where(row % DFT_A_N2 == col % DFT_A_N2, wa, 0.0)

    shape = (DFT_N1, DFT_N2, DFT_N2)
    kk1 = lax.broadcasted_iota(jnp.int32, shape, 0)
    kk2 = lax.broadcasted_iota(jnp.int32, shape, 1)
    nn2 = lax.broadcasted_iota(jnp.int32, shape, 2)
    ang = ((nn2 * (kk1 + DFT_N1 * kk2)) % seq).astype(F32) * (two_pi / seq)
    c, s = jnp.cos(ang), jnp.sin(ang)
    g = jnp.concatenate([jnp.concatenate([c, s], axis=2), jnp.concatenate([-s, c], axis=2)], axis=1)

    m = lax.broadcasted_iota(jnp.int32, (FN_WIDTH, FN_WIDTH), 0)
    j = lax.broadcasted_iota(jnp.int32, (FN_WIDTH, FN_WIDTH), 1)
    same_group = (m // FN_GROUP_DIM) == (j // FN_GROUP_DIM)
    ang = ((m * j) % FN_GROUP_DIM).astype(F32) * (two_pi / FN_GROUP_DIM)
    scale = 1.0 / math.sqrt(seq * FN_GROUP_DIM)
    cc = jnp.where(same_group, jnp.cos(ang) * scale, 0.0)
    sc = jnp.where(same_group, jnp.sin(ang) * scale, 0.0)
    return wa.astype(BF16), g.astype(BF16), cc.astype(BF16), sc.astype(BF16)


DFT_A_N2 = 8
DFT_A_SUB = 2
DFT_C_K1 = 32


def _dft_a_kernel(w_ref, x_ref, o_ref):
    parts = []
    for s in range(DFT_A_SUB):
        x = x_ref[:, s * DFT_A_N2:(s + 1) * DFT_A_N2, :].reshape(DFT_N1 * DFT_A_N2, FN_WIDTH).astype(BF16)
        a = jnp.dot(w_ref[...], x, preferred_element_type=F32)
        parts.append(a.reshape(2, DFT_N1, DFT_A_N2, FN_WIDTH))
    o_ref[...] = jnp.concatenate(parts, axis=2).astype(BF16)


def _dft_c_kernel(ar_ref, ai_ref, g_ref, cc_ref, sc_ref, o_ref, zr_ref, zi_ref):
    for j in range(DFT_C_K1):
        a = jnp.concatenate([ar_ref[j], ai_ref[j]], axis=0)
        z = jnp.dot(g_ref[j], a, preferred_element_type=F32)
        zr_ref[j * DFT_N2:(j + 1) * DFT_N2, :] = z[:DFT_N2].astype(BF16)
        zi_ref[j * DFT_N2:(j + 1) * DFT_N2, :] = z[DFT_N2:].astype(BF16)
    o = (jnp.dot(zr_ref[...], cc_ref[...], preferred_element_type=F32)
         + jnp.dot(zi_ref[...], sc_ref[...], preferred_element_type=F32)).astype(BF16)
    for j in range(DFT_C_K1):
        o_ref[:, j, :] = o[j * DFT_N2:(j + 1) * DFT_N2]


def _fourier_mix(ufn, batch, seq, tables):
    wa, g, cc, sc = tables
    x = ufn.reshape(batch, DFT_N1, DFT_N2, FN_WIDTH)
    nb = DFT_A_N2 * DFT_A_SUB
    a = pl.pallas_call(
        _dft_a_kernel,
        grid=(batch, DFT_N2 // nb),
        in_specs=[_resident((2 * DFT_N1 * DFT_A_N2, DFT_N1 * DFT_A_N2), lambda b, j: (0, 0)),
                  pl.BlockSpec((None, DFT_N1, nb, FN_WIDTH), lambda b, j: (b, 0, j, 0))],
        out_specs=pl.BlockSpec((None, 2, DFT_N1, nb, FN_WIDTH), lambda b, j: (b, 0, 0, j, 0)),
        out_shape=jax.ShapeDtypeStruct((batch, 2, DFT_N1, DFT_N2, FN_WIDTH), BF16),
        compiler_params=_cparams(2),
        name="dft_a",
    )(wa, x)
    kb = DFT_C_K1
    part = lambda p: pl.BlockSpec((None, None, kb, DFT_N2, FN_WIDTH), lambda b, i: (b, p, i, 0, 0))
    out = pl.pallas_call(
        _dft_c_kernel,
        grid=(batch, DFT_N1 // kb),
        in_specs=[part(0), part(1),
                  pl.BlockSpec((kb, 2 * DFT_N2, 2 * DFT_N2), lambda b, i: (i, 0, 0)),
                  _resident((FN_WIDTH, FN_WIDTH), lambda b, i: (0, 0)),
                  _resident((FN_WIDTH, FN_WIDTH), lambda b, i: (0, 0))],
        out_specs=pl.BlockSpec((None, DFT_N2, kb, FN_WIDTH), lambda b, i: (b, 0, i, 0)),
        out_shape=jax.ShapeDtypeStruct((batch, DFT_N2, DFT_N1, FN_WIDTH), BF16),
        scratch_shapes=[pltpu.VMEM((kb * DFT_N2, FN_WIDTH), BF16), pltpu.VMEM((kb * DFT_N2, FN_WIDTH), BF16)],
        compiler_params=_cparams(2),
        name="dft_c",
    )(a, a, g, cc, sc)
    return out.reshape(batch * seq, FN_WIDTH)


NA_DR = 2 * NA_KR - 1


def _na_bias_slabs(rpb):
    depth = rpb.shape[0]
    pad = GRID_W - NA_KC
    period = 2 * GRID_W - 1
    ext = jnp.pad(rpb * LOG2_E, ((0, 0), (0, 0), (0, 0), (pad, pad)))
    flat = jnp.tile(ext, (1, 1, 1, GRID_W))[..., GRID_W - 1:GRID_W - 1 + GRID_W * (period - 1)]
    slab = flat.reshape(depth, NA_HEADS, NA_DR, GRID_W, period - 1)[..., :GRID_W]
    qc, kc = np.arange(GRID_W)[:, None], np.arange(GRID_W)[None, :]
    col_start = np.clip(qc - NA_KC // 2, 0, GRID_W - NA_KC)
    col_ok = (kc >= col_start) & (kc < col_start + NA_KC)
    slab = jnp.where(col_ok, slab, NEG_INF)
    return jnp.concatenate([slab[:, :, :-1], slab[:, :, 1:]], axis=-1)


def _na_row_masks(rows):
    kr = min(NA_KR, rows)
    n_blocks = rows // NA_RQ
    out = np.zeros((3, 2, 3, NA_QB, 2 * NA_HEAD_DIM), np.float32)
    for v, blk in enumerate((0, 1, n_blocks - 1)):
        r = blk * NA_RQ + np.arange(NA_RQ)[:, None]
        key_row = (blk - 1) * NA_RQ + np.arange(3 * NA_RQ)[None, :]
        row_start = np.clip(r - kr // 2, 0, rows - kr)
        row_ok = (key_row >= row_start) & (key_row < row_start + kr)
        pen = np.where(row_ok, 0.0, NEG_INF).reshape(NA_RQ, 3, NA_RQ)
        pen = np.repeat(pen.transpose(1, 2, 0), GRID_W, axis=1)
        for e in range(2):
            lo = NA_HEAD_DIM * (1 - e)
            out[v, e, :, :, lo:lo + NA_RQ] = pen
    return out


def _na_kernel(q_ref, kp_ref, kc_ref, kn_ref, vp_ref, vc_ref, vn_ref, slab_ref, rmask_ref, o_ref):
    lane = lax.broadcasted_iota(jnp.int32, (1, 2 * NA_HEAD_DIM), 1)
    first = lane < NA_HEAD_DIM
    q_row = lax.broadcasted_iota(jnp.int32, (NA_QB, 2 * NA_HEAD_DIM), 0) // GRID_W
    q_lane = lax.broadcasted_iota(jnp.int32, (NA_QB, 2 * NA_HEAD_DIM), 1)
    k_refs = (kp_ref, kc_ref, kn_ref)
    v_refs = (vp_ref, vc_ref, vn_ref)
    for hp in range(NA_HEADS // 2):
        sl = slice(2 * NA_HEAD_DIM * hp, 2 * NA_HEAD_DIM * (hp + 1))
        q2 = q_ref[:, sl]
        acc = []
        for e in range(2):
            mine = first if e == 0 else jnp.logical_not(first)
            row_onehot = (q_lane == NA_HEAD_DIM * (1 - e) + q_row).astype(BF16)
            qm = jnp.where(mine, q2, row_onehot)
            ms, rs = [], []
            for i in range(3):
                km = jnp.where(mine, k_refs[i][:, sl], rmask_ref[e, i])
                s = lax.dot_general(qm, km, (((1,), (1,)), ((), ())), preferred_element_type=F32)
                bias = jnp.concatenate(
                    [jnp.concatenate([slab_ref[2 * hp + e, NA_RQ * i + 2 * m - a + NA_KR - 1 - NA_RQ]
                                      for m in range(NA_RQ // 2)], axis=1) for a in range(NA_RQ)], axis=0)
                s = s + bias
                m = jnp.max(s, axis=-1, keepdims=True)
                p = jnp.exp2(s - m).astype(BF16)
                v2 = v_refs[i][:, sl]
                vm = jnp.where(mine, v2, jnp.ones_like(v2))
                ms.append(m)
                rs.append(jnp.dot(p, vm, preferred_element_type=F32))
            m_all = jnp.maximum(jnp.maximum(ms[0], ms[1]), ms[2])
            acc.append(rs[0] * jnp.exp2(ms[0] - m_all) + rs[1] * jnp.exp2(ms[1] - m_all)
                       + rs[2] * jnp.exp2(ms[2] - m_all))
        num = jnp.where(first, acc[0], acc[1])
        den = pltpu.roll(jnp.where(first, acc[1], acc[0]), NA_HEAD_DIM, 1)
        o_ref[:, sl] = (num / den).astype(BF16)


def _neighbourhood_attention(q, k, v, slabs, row_masks, layer, batch, seq):
    n_blocks = seq // NA_QB
    q3, k3, v3 = (t.reshape(batch, seq, NA_WIDTH) for t in (q, k, v))
    blk = lambda shift: pl.BlockSpec(
        (None, NA_QB, NA_WIDTH), lambda b, i: (b, jnp.clip(i + shift, 0, n_blocks - 1), 0))
    variant = lambda b, i: (jnp.where(i == 0, 0, jnp.where(i == n_blocks - 1, 2, 1)), 0, 0, 0, 0)
    out = pl.pallas_call(
        _na_kernel,
        grid=(batch, n_blocks),
        in_specs=[blk(0), blk(-1), blk(0), blk(1), blk(-1), blk(0), blk(1),
                  _resident((None, NA_HEADS, NA_DR - 1, GRID_W, 2 * GRID_W), lambda b, i: (layer, 0, 0, 0, 0)),
                  pl.BlockSpec((None, 2, 3, NA_QB, 2 * NA_HEAD_DIM), variant)],
        out_specs=blk(0),
        out_shape=jax.ShapeDtypeStruct((batch, seq, NA_WIDTH), BF16),
        compiler_params=_cparams(2),
        name="natten",
    )(q3, k3, k3, k3, v3, v3, v3, slabs, row_masks)
    return out.reshape(batch * seq, NA_WIDTH)


def _cmul(ar, ai, br, bi):
    return ar * br - ai * bi, ar * bi + ai * br


def _ssm_table_kernel(pr_ref, pi_ref, bbr_ref, bbi_ref, cr_ref, ci_ref, intra_ref, sp_ref, qt_ref):
    n, c = SSM_CHUNK, SSM_GROUP_DIM
    hi = lax.Precision.HIGHEST
    lanes_contract = (((1,), (1,)), ((), ()))
    sp_ref[...] = jnp.zeros(sp_ref.shape, sp_ref.dtype)
    qt_ref[...] = jnp.zeros(qt_ref.shape, qt_ref.dtype)
    spread = (lax.broadcasted_iota(jnp.int32, (c, SSM_CK), 1) % c
              == lax.broadcasted_iota(jnp.int32, (c, SSM_CK), 0)).astype(F32)
    delta = (lax.broadcasted_iota(jnp.int32, (SSM_CK, SSM_CK), 1) // c
             - lax.broadcasted_iota(jnp.int32, (SSM_CK, SSM_CK), 0) // c)
    for e in range(2):
        toeplitz = []
        for d in range(2):
            pr, pi = pr_ref[d, e], pi_ref[d, e]
            w = [_cmul(bbr_ref[d, e], bbi_ref[d, e], pr[t:t + 1], pi[t:t + 1]) for t in range(n)]
            wr = jnp.concatenate([x[0] for x in w], axis=0)
            wi = jnp.concatenate([x[1] for x in w], axis=0)
            kt = (lax.dot_general(wr, cr_ref[d, e], lanes_contract, precision=hi, preferred_element_type=F32)
                  - lax.dot_general(wi, ci_ref[d, e], lanes_contract, precision=hi, preferred_element_type=F32))
            toeplitz.append(jnp.dot(kt, spread, precision=hi, preferred_element_type=F32))
            order = range(n - 1, -1, -1) if d == 0 else range(n)
            for ri in range(2):
                col = (2 * d + ri) * 2 * SSM_STATE + e * SSM_STATE
                sp_ref[e, :, col:col + SSM_STATE] = jnp.concatenate([w[t][ri] for t in order], axis=0).astype(BF16)
            exps = range(1, n + 1) if d == 0 else range(n, 0, -1)
            q = [_cmul(cr_ref[d, e], ci_ref[d, e], pr[m:m + 1], pi[m:m + 1]) for m in exps]
            rows = slice(e * SSM_CK, (e + 1) * SSM_CK)
            cols = slice(e * SSM_STATE, (e + 1) * SSM_STATE)
            qt_ref[2 * d, rows, cols] = jnp.concatenate([x[0] for x in q], axis=0).astype(BF16)
            qt_ref[2 * d + 1, rows, cols] = (-jnp.concatenate([x[1] for x in q], axis=0)).astype(BF16)
        fwd, bwd = toeplitz
        acc = jnp.zeros((SSM_CK, SSM_CK), F32)
        for dlt in range(-(n - 1), n):
            if dlt > 0:
                src = fwd[dlt * c:(dlt + 1) * c]
            elif dlt < 0:
                src = bwd[-dlt * c:(-dlt + 1) * c]
            else:
                src = fwd[:c] + bwd[:c]
            acc = jnp.where(delta == dlt, jnp.concatenate([src] * n, axis=0), acc)
        intra_ref[e] = acc.astype(BF16)


def _ssm_tables(a_re, a_im, log_dt, b_re, b_im, c_re, c_im):
    depth = a_re.shape[0]
    dt = jnp.exp(log_dt)[..., None]
    zr, zi = a_re * dt, a_im * dt

    def powers(taus):
        taus = taus.astype(F32)[:, None]
        mag = jnp.exp(zr[..., None, :] * taus)
        return mag * jnp.cos(zi[..., None, :] * taus), mag * jnp.sin(zi[..., None, :] * taus)

    pr, pi = powers(jnp.arange(SSM_CHUNK + 1))
    lbr, lbi = pr[..., 1, :], pi[..., 1, :]
    den = a_re * a_re + a_im * a_im
    fr = ((lbr - 1.0) * a_re + lbi * a_im) / den
    fi = (lbi * a_re - (lbr - 1.0) * a_im) / den
    bbr, bbi = _cmul(fr[..., None, :], fi[..., None, :],
                     jnp.swapaxes(b_re, -1, -2), jnp.swapaxes(b_im, -1, -2))

    pair = lambda rows: pl.BlockSpec((None, 2, 2, rows, SSM_STATE), lambda l, p: (l, 0, p, 0, 0))
    intra, sproj, qproj_t = pl.pallas_call(
        _ssm_table_kernel,
        grid=(depth, SSM_PAIRS),
        in_specs=[pair(SSM_CHUNK + 1), pair(SSM_CHUNK + 1), pair(SSM_GROUP_DIM), pair(SSM_GROUP_DIM),
                  pair(SSM_GROUP_DIM), pair(SSM_GROUP_DIM)],
        out_specs=[pl.BlockSpec((None, 2, SSM_CK, SSM_CK), lambda l, p: (l, p, 0, 0)),
                   pl.BlockSpec((None, None, 2, SSM_CK, 8 * SSM_STATE), lambda l, p: (l, p, 0, 0, 0)),
                   pl.BlockSpec((None, None, 4, 2 * SSM_CK, 2 * SSM_STATE), lambda l, p: (l, p, 0, 0, 0))],
        out_shape=[jax.ShapeDtypeStruct((depth, SSM_GROUPS, SSM_CK, SSM_CK), BF16),
                   jax.ShapeDtypeStruct((depth, SSM_PAIRS, 2, SSM_CK, 8 * SSM_STATE), BF16),
                   jax.ShapeDtypeStruct((depth, SSM_PAIRS, 4, 2 * SSM_CK, 2 * SSM_STATE), BF16)],
        compiler_params=_cparams(2),
        name="ssm_tables",
    )(pr, pi, bbr, bbi, c_re, c_im)

    sr, si = powers(jnp.arange(1, 9) * SSM_CHUNK)
    sr, si = sr.transpose(0, 1, 3, 2, 4), si.transpose(0, 1, 3, 2, 4)
    lam_pows = jnp.stack([jnp.stack([sr[:, 0], si[:, 0]], axis=1),
                          jnp.stack([sr[:, 1, ::-1], si[:, 1, ::-1]], axis=1)], axis=1)
    lam_pows = lam_pows.reshape(depth, 4 * 8, SSM_GROUPS * SSM_STATE)
    return intra, sproj, qproj_t, lam_pows


SSM_ROW_TILE = 1024


def _ssm_chunk_states(u_ref, sp_ref, st_ref):
    for r in range(u_ref.shape[1] // SSM_ROW_TILE):
        rows = slice(r * SSM_ROW_TILE, (r + 1) * SSM_ROW_TILE)
        st = (jnp.dot(u_ref[0, rows, :], sp_ref[0], preferred_element_type=F32)
              + jnp.dot(u_ref[1, rows, :], sp_ref[1], preferred_element_type=F32))
        for part in range(4):
            st_ref[part, rows, :] = st[:, 128 * part:128 * (part + 1)]


def _ssm_carry_states(st_ref, lam_ref, xin_ref, n_seq):
    n_chunks, lanes = st_ref.shape[1] // n_seq, st_ref.shape[2]
    n_tiles = n_chunks // 8
    row = lax.broadcasted_iota(jnp.int32, (8, lanes), 0)

    def bcast(x, r):
        return jnp.broadcast_to(x[r:r + 1, :], (8, lanes))

    tabs = []
    for d in range(2):
        pr, pi = lam_ref[16 * d:16 * d + 8, :], lam_ref[16 * d + 8:16 * d + 16, :]
        at = (lambda m: m - 1) if d == 0 else (lambda m: 8 - m)
        tabs.append((pr, pi, [(bcast(pr, at(m)), bcast(pi, at(m))) for m in (1, 2, 4)]))

    def shifted(x, dist, forward):
        if forward:
            return jnp.where(row >= dist, pltpu.roll(x, dist, 0), 0.0)
        return jnp.where(row < 8 - dist, pltpu.roll(x, 8 - dist, 0), 0.0)

    def scan_tile(sr, si, cr, ci, d):
        forward = d == 0
        pr, pi, steps = tabs[d]
        hr, hi = sr, si
        for (lr, li), dist in zip(steps, (1, 2, 4)):
            mr, mi = _cmul(lr, li, shifted(hr, dist, forward), shifted(hi, dist, forward))
            hr, hi = hr + mr, hi + mi
        mr, mi = _cmul(pr, pi, cr, ci)
        er, ei = hr + mr, hi + mi
        if forward:
            xr = jnp.where(row >= 1, pltpu.roll(er, 1, 0), cr)
            xi = jnp.where(row >= 1, pltpu.roll(ei, 1, 0), ci)
            return xr, xi, bcast(er, 7), bcast(ei, 7)
        xr = jnp.where(row < 7, pltpu.roll(er, 7, 0), cr)
        xi = jnp.where(row < 7, pltpu.roll(ei, 7, 0), ci)
        return xr, xi, bcast(er, 0), bcast(ei, 0)

    def body(i, carry):
        new = []
        for s in range(n_seq):
            for d in range(2):
                cr, ci = carry[2 * (2 * s + d)], carry[2 * (2 * s + d) + 1]
                r0 = pl.multiple_of(s * n_chunks + (i if d == 0 else n_tiles - 1 - i) * 8, 8)
                sr, si = st_ref[2 * d, pl.ds(r0, 8), :], st_ref[2 * d + 1, pl.ds(r0, 8), :]
                xr, xi, ncr, nci = scan_tile(sr, si, cr, ci, d)
                xin_ref[2 * d, pl.ds(r0, 8), :] = xr
                xin_ref[2 * d + 1, pl.ds(r0, 8), :] = xi
                new += [ncr, nci]
        return tuple(new)

    zero = jnp.zeros((8, lanes), F32)
    lax.fori_loop(0, n_tiles, body, (zero,) * (4 * n_seq))


def _ssm_outputs(u_ref, xin_ref, mi_ref, qt_ref, o_ref):
    lanes_contract = (((1,), (1,)), ((), ()))
    for r in range(u_ref.shape[1] // SSM_ROW_TILE):
        rows = slice(r * SSM_ROW_TILE, (r + 1) * SSM_ROW_TILE)
        acc = lax.dot_general(xin_ref[0, rows, :].astype(BF16), qt_ref[0], lanes_contract,
                              preferred_element_type=F32)
        for part in range(1, 4):
            acc = acc + lax.dot_general(xin_ref[part, rows, :].astype(BF16), qt_ref[part], lanes_contract,
                                        preferred_element_type=F32)
        for e in range(2):
            y0 = jnp.dot(u_ref[e, rows, :], mi_ref[e], preferred_element_type=F32)
            o_ref[e, rows, :] = y0 + acc[:, SSM_CK * e:SSM_CK * (e + 1)]


def _ssm_pair_kernel(u_ref, sp_ref, lam_ref, mi_ref, qt_ref, o_ref, st_ref, xin_ref, *, n_seq):
    _ssm_chunk_states(u_ref, sp_ref, st_ref)
    _ssm_carry_states(st_ref, lam_ref, xin_ref, n_seq)
    _ssm_outputs(u_ref, xin_ref, mi_ref, qt_ref, o_ref)


def _ssm_scan(upack, batch, seq, tables, layer):
    intra, sproj, qproj_t, lam_pows = tables
    rows = (seq // SSM_CHUNK) * batch
    lanes = 2 * SSM_STATE
    pair_rows = pl.BlockSpec((2, rows, SSM_CK), lambda p: (p, 0, 0))
    return pl.pallas_call(
        functools.partial(_ssm_pair_kernel, n_seq=batch),
        grid=(SSM_PAIRS,),
        in_specs=[pair_rows,
                  pl.BlockSpec((None, None, 2, SSM_CK, 4 * lanes), lambda p: (layer, p, 0, 0, 0)),
                  pl.BlockSpec((None, 32, lanes), lambda p: (layer, 0, p)),
                  pl.BlockSpec((None, 2, SSM_CK, SSM_CK), lambda p: (layer, p, 0, 0)),
                  pl.BlockSpec((None, None, 4, 2 * SSM_CK, lanes), lambda p: (layer, p, 0, 0, 0))],
        out_specs=pair_rows,
        out_shape=jax.ShapeDtypeStruct((SSM_GROUPS, rows, SSM_CK), F32),
        scratch_shapes=[pltpu.VMEM((4, rows, lanes), F32), pltpu.VMEM((4, rows, lanes), F32)],
        compiler_params=_cparams(1),
        name="ssm_pair",
    )(upack, sproj, lam_pows, intra, qproj_t)


def _merge_tile(x_ref, gate_ref, fm_ref, na_ref, ypack_ref, ussm_ref, dskip_ref, wglu_ref,
                wfn_ref, wna_ref, wssm_ref, wout_ref, ytok_ref, yrow_ref):
    y_fn = jnp.dot(fm_ref[...], wfn_ref[...], preferred_element_type=F32)
    y_na = jnp.dot(na_ref[...], wna_ref[...], preferred_element_type=F32)
    merged = (gate_ref[:, :D_MODEL].astype(F32) * y_fn
              + gate_ref[:, D_MODEL:2 * D_MODEL].astype(F32) * y_na)
    for t in range(SSM_CHUNK):
        for g in range(SSM_GROUPS):
            yrow_ref[:, g * SSM_GROUP_DIM:(g + 1) * SSM_GROUP_DIM] = (
                ypack_ref[g, :, t * SSM_GROUP_DIM:(t + 1) * SSM_GROUP_DIM])
        ytok_ref[:, t, :] = yrow_ref[...]
    yscan = ytok_ref[...].reshape(TOKEN_TILE, SSM_WIDTH)
    y = yscan + dskip_ref[...] * ussm_ref[...]
    y = jax.nn.gelu(y)
    y = y * jax.nn.sigmoid(jnp.dot(y.astype(BF16), wglu_ref[...], preferred_element_type=F32))
    y_ssm = jnp.dot(y.astype(BF16), wssm_ref[...], preferred_element_type=F32)
    merged = merged + gate_ref[:, 2 * D_MODEL:].astype(F32) * y_ssm
    return x_ref[...] + jnp.dot(merged.astype(BF16), wout_ref[...], preferred_element_type=F32)


def _ffn_tile(x, g_ref, wup_ref, wdown_ref, gfin_ref, up_ref, final_norm):
    h = _rms(x, g_ref[...]).astype(BF16)
    for c in range(D_FF // D_MODEL):
        cols = slice(c * D_MODEL, (c + 1) * D_MODEL)
        a = jnp.maximum(jnp.dot(h, wup_ref[:, cols], preferred_element_type=F32), 0.0)
        up_ref[:, cols] = (a * a).astype(BF16)
    y = x + jnp.dot(up_ref[...], wdown_ref[...], preferred_element_type=F32)
    if final_norm:
        y = _rms(y, gfin_ref[...])
    return y


def _mix_ffn_kernel(x_ref, gate_ref, fm_ref, na_ref, ypack_ref, ussm_ref, dskip_ref, wglu_ref,
                    wfn_ref, wna_ref, wssm_ref, wout_ref, g_ref, wup_ref, wdown_ref, gfin_ref,
                    o_ref, ytok_ref, yrow_ref, up_ref, *, final_norm):
    x1 = _merge_tile(x_ref, gate_ref, fm_ref, na_ref, ypack_ref, ussm_ref, dskip_ref, wglu_ref,
                     wfn_ref, wna_ref, wssm_ref, wout_ref, ytok_ref, yrow_ref)
    o_ref[...] = _ffn_tile(x1, g_ref, wup_ref, wdown_ref, gfin_ref, up_ref, final_norm)


def _mix_ffn(x, gates, fm, na, ypack, ussm, ssm_d, w_glu, w_br_fn, w_br_na, w_br_ssm, w_out,
             g_ffn, w_up, w_down, g_final, layer, final_norm):
    n = x.shape[0]
    tm = TOKEN_TILE
    row = lambda width: pl.BlockSpec((tm, width), lambda i: (i, 0))
    wspec = lambda k, m: _resident((None, k, m), lambda i: (layer, 0, 0))
    return pl.pallas_call(
        functools.partial(_mix_ffn_kernel, final_norm=final_norm),
        grid=(n // tm,),
        in_specs=[row(D_MODEL), row(N_BRANCH * D_MODEL), row(FN_WIDTH), row(NA_WIDTH),
                  pl.BlockSpec((SSM_GROUPS, tm // SSM_CHUNK, SSM_CK), lambda i: (0, i, 0)),
                  row(SSM_WIDTH), wspec(1, SSM_WIDTH), wspec(SSM_WIDTH, SSM_WIDTH),
                  wspec(FN_WIDTH, D_MODEL), wspec(NA_WIDTH, D_MODEL), wspec(SSM_WIDTH, D_MODEL),
                  wspec(D_MODEL, D_MODEL),
                  wspec(1, D_MODEL), wspec(D_MODEL, D_FF), wspec(D_FF, D_MODEL),
                  _resident((1, D_MODEL), lambda i: (0, 0))],
        out_specs=row(D_MODEL),
        out_shape=jax.ShapeDtypeStruct((n, D_MODEL), F32),
        scratch_shapes=[pltpu.VMEM((tm // SSM_CHUNK, SSM_CHUNK, SSM_WIDTH), F32),
                        pltpu.VMEM((tm // SSM_CHUNK, SSM_WIDTH), F32),
                        pltpu.VMEM((tm, D_FF), BF16)],
        compiler_params=_cparams(1),
        name="mix_ffn",
    )(x, gates, fm, na, ypack, ussm, ssm_d, w_glu, w_br_fn, w_br_na, w_br_ssm, w_out,
      g_ffn, w_up, w_down, g_final)


def kernel(x, g_mix, w_in, na_rpb, ssm_a_re, ssm_a_im, ssm_log_dt, ssm_b_re, ssm_b_im, ssm_c_re, ssm_c_im,
           ssm_d, w_glu, w_br_fn, w_br_na, w_br_ssm, w_out, g_ffn, w_up, w_down, g_final):
    batch, seq, _ = x.shape
    depth = w_in.shape[0]
    assert seq == DFT_N1 * DFT_N2 and seq % (GRID_W * NA_RQ) == 0 and (batch * seq) % TOKEN_TILE == 0
    rows = seq // GRID_W

    to_bf16 = lambda w: w.astype(BF16)
    w_in, w_glu, w_br_fn, w_br_na, w_br_ssm, w_out, w_up, w_down = map(
        to_bf16, (w_in, w_glu, w_br_fn, w_br_na, w_br_ssm, w_out, w_up, w_down))
    g_mix3 = g_mix.reshape(depth, 1, D_MODEL)
    g_ffn3 = g_ffn.reshape(depth, 1, D_MODEL)
    ssm_d3 = ssm_d.reshape(depth, 1, SSM_WIDTH)
    g_final2 = g_final.reshape(1, D_MODEL)
    dft_tables = _dft_tables(seq)
    na_slabs = _na_bias_slabs(na_rpb)
    na_row_masks = jnp.asarray(_na_row_masks(rows)).astype(BF16)
    ssm_tables = _ssm_tables(ssm_a_re, ssm_a_im, ssm_log_dt, ssm_b_re, ssm_b_im, ssm_c_re, ssm_c_im)

    xs = x.reshape(batch * seq, D_MODEL)
    for l in range(depth):
        ufn, q, k, v, ussm, upack, gates = _inproj(xs, g_mix3, w_in, l)
        fm = _fourier_mix(ufn, batch, seq, dft_tables)
        na = _neighbourhood_attention(q, k, v, na_slabs, na_row_masks, l, batch, seq)
        ypack = _ssm_scan(upack, batch, seq, ssm_tables, l)
        xs = _mix_ffn(xs, gates, fm, na, ypack, ussm, ssm_d3, w_glu, w_br_fn, w_br_na, w_br_ssm, w_out,
                      g_ffn3, w_up, w_down, g_final2, l, final_norm=(l == depth - 1))
    return xs.reshape(batch, seq, D_MODEL)
```

```python
import functools
import math

import numpy as np
import jax
import jax.numpy as jnp
from jax import lax
from jax.experimental import pallas as pl
from jax.experimental.pallas import tpu as pltpu

F32 = jnp.float32
BF16 = jnp.bfloat16

D_MODEL = 1024
GRID_W = 64
FN_GROUP_DIM = 64
FN_WIDTH = 256
NA_HEADS = 8
NA_HEAD_DIM = 64
NA_WIDTH = 512
NA_KR = 8
NA_KC = 16
NEG_INF = -1e30
SSM_GROUPS = 16
SSM_GROUP_DIM = 16
SSM_WIDTH = 256
SSM_STATE = 64
N_BRANCH = 3
D_FF = 4 * D_MODEL
D_IN = FN_WIDTH + 3 * NA_WIDTH + SSM_WIDTH + N_BRANCH * D_MODEL
RMS_EPS = 1e-6
LOG2_E = 1.0 / math.log(2.0)

DFT_N1 = 128
DFT_N2 = 64
NA_RQ = 4
NA_QB = NA_RQ * GRID_W
SSM_CHUNK = 16
SSM_PAIRS = SSM_GROUPS // 2
SSM_CK = SSM_CHUNK * SSM_GROUP_DIM

TOKEN_TILE = 512
INPROJ_TILE = 1024
VMEM_LIMIT = 56 * 1024 * 1024


def _cparams(n_axes):
    return pltpu.CompilerParams(dimension_semantics=("arbitrary",) * n_axes, vmem_limit_bytes=VMEM_LIMIT)


def _resident(block_shape, index_map):
    return pl.BlockSpec(block_shape, index_map, pipeline_mode=pl.Buffered(1))


def _rms(x, g):
    ms = jnp.mean(x * x, axis=-1, keepdims=True)
    return x * lax.rsqrt(ms + RMS_EPS) * g


def _inproj_kernel(x_ref, g_ref, w_ref, ufn_ref, q_ref, k_ref, v_ref, ussm_ref, upack_ref, gate_ref):
    h = _rms(x_ref[...], g_ref[...]).astype(BF16)

    def seg(lo, width):
        return jnp.dot(h, w_ref[:, lo:lo + width], preferred_element_type=F32)

    ufn_ref[...] = seg(0, FN_WIDTH)
    q_ref[...] = (seg(FN_WIDTH, NA_WIDTH) * (NA_HEAD_DIM ** -0.5 * LOG2_E)).astype(BF16)
    k_ref[...] = seg(FN_WIDTH + NA_WIDTH, NA_WIDTH).astype(BF16)
    v_ref[...] = seg(FN_WIDTH + 2 * NA_WIDTH, NA_WIDTH).astype(BF16)
    ussm = seg(FN_WIDTH + 3 * NA_WIDTH, SSM_WIDTH)
    ussm_ref[...] = ussm
    u3 = ussm.astype(BF16).reshape(INPROJ_TILE // SSM_CHUNK, SSM_CHUNK, SSM_WIDTH)
    for t in range(SSM_CHUNK):
        ut = u3[:, t, :]
        for g in range(SSM_GROUPS):
            upack_ref[g, :, t * SSM_GROUP_DIM:(t + 1) * SSM_GROUP_DIM] = (
                ut[:, g * SSM_GROUP_DIM:(g + 1) * SSM_GROUP_DIM])
    gate_lo = FN_WIDTH + 3 * NA_WIDTH + SSM_WIDTH
    for j in range(N_BRANCH):
        gate_ref[:, j * D_MODEL:(j + 1) * D_MODEL] = jax.nn.sigmoid(seg(gate_lo + j * D_MODEL, D_MODEL)).astype(BF16)


def _inproj(x, g_mix, w_in, layer):
    n = x.shape[0]
    tm = INPROJ_TILE
    row = lambda width: pl.BlockSpec((tm, width), lambda i: (i, 0))
    return pl.pallas_call(
        _inproj_kernel,
        grid=(n // tm,),
        in_specs=[row(D_MODEL),
                  _resident((None, 1, D_MODEL), lambda i: (layer, 0, 0)),
                  _resident((None, D_MODEL, D_IN), lambda i: (layer, 0, 0))],
        out_specs=[row(FN_WIDTH), row(NA_WIDTH), row(NA_WIDTH), row(NA_WIDTH), row(SSM_WIDTH),
                   pl.BlockSpec((SSM_GROUPS, tm // SSM_CHUNK, SSM_CK), lambda i: (0, i, 0)),
                   row(N_BRANCH * D_MODEL)],
        out_shape=[jax.ShapeDtypeStruct((n, FN_WIDTH), F32),
                   jax.ShapeDtypeStruct((n, NA_WIDTH), BF16),
                   jax.ShapeDtypeStruct((n, NA_WIDTH), BF16),
                   jax.ShapeDtypeStruct((n, NA_WIDTH), BF16),
                   jax.ShapeDtypeStruct((n, SSM_WIDTH), F32),
                   jax.ShapeDtypeStruct((SSM_GROUPS, n // SSM_CHUNK, SSM_CK), BF16),
                   jax.ShapeDtypeStruct((n, N_BRANCH * D_MODEL), BF16)],
        compiler_params=_cparams(1),
        name="inproj",
    )(x, g_mix, w_in)


def _dft_tables(seq):
    two_pi = 2.0 * math.pi
    shape = (2 * DFT_N1 * DFT_A_N2, DFT_N1 * DFT_A_N2)
    row = lax.broadcasted_iota(jnp.int32, shape, 0)
    col = lax.broadcasted_iota(jnp.int32, shape, 1)
    k1, n1 = (row // DFT_A_N2) % DFT_N1, col // DFT_A_N2
    ang = ((k1 * n1) % DFT_N1).astype(F32) * (two_pi / DFT_N1)
    wa = jnp.where(row < DFT_N1 * DFT_A_N2, jnp.cos(ang), -jnp.sin(ang))
    wa = jnp.where(row % DFT_A_N2 == col % DFT_A_N2, wa, 0.0)

    shape = (DFT_N1, DFT_N2, DFT_N2)
    kk1 = lax.broadcasted_iota(jnp.int32, shape, 0)
    kk2 = lax.broadcasted_iota(jnp.int32, shape, 1)
    nn2 = lax.broadcasted_iota(jnp.int32, shape, 2)
    ang = ((nn2 * (kk1 + DFT_N1 * kk2)) % seq).astype(F32) * (two_pi / seq)
    c, s = jnp.cos(ang), jnp.sin(ang)
    g = jnp.concatenate([jnp.concatenate([c, s], axis=2), jnp.concatenate([-s, c], axis=2)], axis=1)

    m = lax.broadcasted_iota(jnp.int32, (FN_WIDTH, FN_WIDTH), 0)
    j = lax.broadcasted_iota(jnp.int32, (FN_WIDTH, FN_WIDTH), 1)
    same_group = (m // FN_GROUP_DIM) == (j // FN_GROUP_DIM)
    ang = ((m * j) % FN_GROUP_DIM).astype(F32) * (two_pi / FN_GROUP_DIM)
    scale = 1.0 / math.sqrt(seq * FN_GROUP_DIM)
    cc = jnp.where(same_group, jnp.cos(ang) * scale, 0.0)
    sc = jnp.where(same_group, jnp.sin(ang) * scale, 0.0)
    return wa.astype(BF16), g.astype(BF16), cc.astype(BF16), sc.astype(BF16)


DFT_A_N2 = 8
DFT_A_SUB = 2
DFT_C_K1 = 32


def _dft_a_kernel(w_ref, x_ref, o_ref):
    parts = []
    for s in range(DFT_A_SUB):
        x = x_ref[:, s * DFT_A_N2:(s + 1) * DFT_A_N2, :].reshape(DFT_N1 * DFT_A_N2, FN_WIDTH).astype(BF16)
        a = jnp.dot(w_ref[...], x, preferred_element_type=F32)
        parts.append(a.reshape(2, DFT_N1, DFT_A_N2, FN_WIDTH))
    o_ref[...] = jnp.concatenate(parts, axis=2).astype(BF16)


def _dft_c_kernel(ar_ref, ai_ref, g_ref, cc_ref, sc_ref, o_ref, zr_ref, zi_ref):
    for j in range(DFT_C_K1):
        a = jnp.concatenate([ar_ref[j], ai_ref[j]], axis=0)
        z = jnp.dot(g_ref[j], a, preferred_element_type=F32)
        zr_ref[j * DFT_N2:(j + 1) * DFT_N2, :] = z[:DFT_N2].astype(BF16)
        zi_ref[j * DFT_N2:(j + 1) * DFT_N2, :] = z[DFT_N2:].astype(BF16)
    o = (jnp.dot(zr_ref[...], cc_ref[...], preferred_element_type=F32)
         + jnp.dot(zi_ref[...], sc_ref[...], preferred_element_type=F32)).astype(BF16)
    for j in range(DFT_C_K1):
        o_ref[:, j, :] = o[j * DFT_N2:(j + 1) * DFT_N2]


def _fourier_mix(ufn, batch, seq, tables):
    wa, g, cc, sc = tables
    x = ufn.reshape(batch, DFT_N1, DFT_N2, FN_WIDTH)
    nb = DFT_A_N2 * DFT_A_SUB
    a = pl.pallas_call(
        _dft_a_kernel,
        grid=(batch, DFT_N2 // nb),
        in_specs=[_resident((2 * DFT_N1 * DFT_A_N2, DFT_N1 * DFT_A_N2), lambda b, j: (0, 0)),
                  pl.BlockSpec((None, DFT_N1, nb, FN_WIDTH), lambda b, j: (b, 0, j, 0))],
        out_specs=pl.BlockSpec((None, 2, DFT_N1, nb, FN_WIDTH), lambda b, j: (b, 0, 0, j, 0)),
        out_shape=jax.ShapeDtypeStruct((batch, 2, DFT_N1, DFT_N2, FN_WIDTH), BF16),
        compiler_params=_cparams(2),
        name="dft_a",
    )(wa, x)
    kb = DFT_C_K1
    part = lambda p: pl.BlockSpec((None, None, kb, DFT_N2, FN_WIDTH), lambda b, i: (b, p, i, 0, 0))
    out = pl.pallas_call(
        _dft_c_kernel,
        grid=(batch, DFT_N1 // kb),
        in_specs=[part(0), part(1),
                  pl.BlockSpec((kb, 2 * DFT_N2, 2 * DFT_N2), lambda b, i: (i, 0, 0)),
                  _resident((FN_WIDTH, FN_WIDTH), lambda b, i: (0, 0)),
                  _resident((FN_WIDTH, FN_WIDTH), lambda b, i: (0, 0))],
        out_specs=pl.BlockSpec((None, DFT_N2, kb, FN_WIDTH), lambda b, i: (b, 0, i, 0)),
        out_shape=jax.ShapeDtypeStruct((batch, DFT_N2, DFT_N1, FN_WIDTH), BF16),
        scratch_shapes=[pltpu.VMEM((kb * DFT_N2, FN_WIDTH), BF16), pltpu.VMEM((kb * DFT_N2, FN_WIDTH), BF16)],
        compiler_params=_cparams(2),
        name="dft_c",
    )(a, a, g, cc, sc)
    return out.reshape(batch * seq, FN_WIDTH)


NA_DR = 2 * NA_KR - 1


def _na_bias_slabs(rpb):
    depth = rpb.shape[0]
    pad = GRID_W - NA_KC
    period = 2 * GRID_W - 1
    ext = jnp.pad(rpb * LOG2_E, ((0, 0), (0, 0), (0, 0), (pad, pad)))
    flat = jnp.tile(ext, (1, 1, 1, GRID_W))[..., GRID_W - 1:GRID_W - 1 + GRID_W * (period - 1)]
    slab = flat.reshape(depth, NA_HEADS, NA_DR, GRID_W, period - 1)[..., :GRID_W]
    qc, kc = np.arange(GRID_W)[:, None], np.arange(GRID_W)[None, :]
    col_start = np.clip(qc - NA_KC // 2, 0, GRID_W - NA_KC)
    col_ok = (kc >= col_start) & (kc < col_start + NA_KC)
    slab = jnp.where(col_ok, slab, NEG_INF)
    return jnp.concatenate([slab[:, :, :-1], slab[:, :, 1:]], axis=-1)


def _na_row_masks(rows):
    kr = min(NA_KR, rows)
    n_blocks = rows // NA_RQ
    out = np.zeros((3, 2, 3, NA_QB, 2 * NA_HEAD_DIM), np.float32)
    for v, blk in enumerate((0, 1, n_blocks - 1)):
        r = blk * NA_RQ + np.arange(NA_RQ)[:, None]
        key_row = (blk - 1) * NA_RQ + np.arange(3 * NA_RQ)[None, :]
        row_start = np.clip(r - kr // 2, 0, rows - kr)
        row_ok = (key_row >= row_start) & (key_row < row_start + kr)
        pen = np.where(row_ok, 0.0, NEG_INF).reshape(NA_RQ, 3, NA_RQ)
        pen = np.repeat(pen.transpose(1, 2, 0), GRID_W, axis=1)
        for e in range(2):
            lo = NA_HEAD_DIM * (1 - e)
            out[v, e, :, :, lo:lo + NA_RQ] = pen
    return out


def _na_kernel(q_ref, kp_ref, kc_ref, kn_ref, vp_ref, vc_ref, vn_ref, slab_ref, rmask_ref, o_ref):
    lane = lax.broadcasted_iota(jnp.int32, (1, 2 * NA_HEAD_DIM), 1)
    first = lane < NA_HEAD_DIM
    q_row = lax.broadcasted_iota(jnp.int32, (NA_QB, 2 * NA_HEAD_DIM), 0) // GRID_W
    q_lane = lax.broadcasted_iota(jnp.int32, (NA_QB, 2 * NA_HEAD_DIM), 1)
    k_refs = (kp_ref, kc_ref, kn_ref)
    v_refs = (vp_ref, vc_ref, vn_ref)
    for hp in range(NA_HEADS // 2):
        sl = slice(2 * NA_HEAD_DIM * hp, 2 * NA_HEAD_DIM * (hp + 1))
        q2 = q_ref[:, sl]
        acc = []
        for e in range(2):
            mine = first if e == 0 else jnp.logical_not(first)
            row_onehot = (q_lane == NA_HEAD_DIM * (1 - e) + q_row).astype(BF16)
            qm = jnp.where(mine, q2, row_onehot)
            ms, rs = [], []
            for i in range(3):
                km = jnp.where(mine, k_refs[i][:, sl], rmask_ref[e, i])
                s = lax.dot_general(qm, km, (((1,), (1,)), ((), ())), preferred_element_type=F32)
                bias = jnp.concatenate(
                    [jnp.concatenate([slab_ref[2 * hp + e, NA_RQ * i + 2 * m - a + NA_KR - 1 - NA_RQ]
                                      for m in range(NA_RQ // 2)], axis=1) for a in range(NA_RQ)], axis=0)
                s = s + bias
                m = jnp.max(s, axis=-1, keepdims=True)
                p = jnp.exp2((s - m).astype(BF16))
                v2 = v_refs[i][:, sl]
                vm = jnp.where(mine, v2, jnp.ones_like(v2))
                ms.append(m)
                rs.append(jnp.dot(p, vm, preferred_element_type=F32))
            m_all = jnp.maximum(jnp.maximum(ms[0], ms[1]), ms[2])
            acc.append(rs[0] * jnp.exp2(ms[0] - m_all) + rs[1] * jnp.exp2(ms[1] - m_all)
                       + rs[2] * jnp.exp2(ms[2] - m_all))
        num = jnp.where(first, acc[0], acc[1])
        den = pltpu.roll(jnp.where(first, acc[1], acc[0]), NA_HEAD_DIM, 1)
        o_ref[:, sl] = (num / den).astype(BF16)


def _neighbourhood_attention(q, k, v, slabs, row_masks, layer, batch, seq):
    n_blocks = seq // NA_QB
    q3, k3, v3 = (t.reshape(batch, seq, NA_WIDTH) for t in (q, k, v))
    blk = lambda shift: pl.BlockSpec(
        (None, NA_QB, NA_WIDTH), lambda b, i: (b, jnp.clip(i + shift, 0, n_blocks - 1), 0))
    variant = lambda b, i: (jnp.where(i == 0, 0, jnp.where(i == n_blocks - 1, 2, 1)), 0, 0, 0, 0)
    out = pl.pallas_call(
        _na_kernel,
        grid=(batch, n_blocks),
        in_specs=[blk(0), blk(-1), blk(0), blk(1), blk(-1), blk(0), blk(1),
                  _resident((None, NA_HEADS, NA_DR - 1, GRID_W, 2 * GRID_W), lambda b, i: (layer, 0, 0, 0, 0)),
                  pl.BlockSpec((None, 2, 3, NA_QB, 2 * NA_HEAD_DIM), variant)],
        out_specs=blk(0),
        out_shape=jax.ShapeDtypeStruct((batch, seq, NA_WIDTH), BF16),
        compiler_params=_cparams(2),
        name="natten",
    )(q3, k3, k3, k3, v3, v3, v3, slabs, row_masks)
    return out.reshape(batch * seq, NA_WIDTH)


def _cmul(ar, ai, br, bi):
    return ar * br - ai * bi, ar * bi + ai * br


def _ssm_table_kernel(pr_ref, pi_ref, bbr_ref, bbi_ref, cr_ref, ci_ref, intra_ref, sp_ref, qt_ref):
    n, c = SSM_CHUNK, SSM_GROUP_DIM
    hi = lax.Precision.HIGHEST
    lanes_contract = (((1,), (1,)), ((), ()))
    sp_ref[...] = jnp.zeros(sp_ref.shape, sp_ref.dtype)
    qt_ref[...] = jnp.zeros(qt_ref.shape, qt_ref.dtype)
    spread = (lax.broadcasted_iota(jnp.int32, (c, SSM_CK), 1) % c
              == lax.broadcasted_iota(jnp.int32, (c, SSM_CK), 0)).astype(F32)
    delta = (lax.broadcasted_iota(jnp.int32, (SSM_CK, SSM_CK), 1) // c
             - lax.broadcasted_iota(jnp.int32, (SSM_CK, SSM_CK), 0) // c)
    for e in range(2):
        toeplitz = []
        for d in range(2):
            pr, pi = pr_ref[d, e], pi_ref[d, e]
            w = [_cmul(bbr_ref[d, e], bbi_ref[d, e], pr[t:t + 1], pi[t:t + 1]) for t in range(n)]
            wr = jnp.concatenate([x[0] for x in w], axis=0)
            wi = jnp.concatenate([x[1] for x in w], axis=0)
            kt = (lax.dot_general(wr, cr_ref[d, e], lanes_contract, precision=hi, preferred_element_type=F32)
                  - lax.dot_general(wi, ci_ref[d, e], lanes_contract, precision=hi, preferred_element_type=F32))
            toeplitz.append(jnp.dot(kt, spread, precision=hi, preferred_element_type=F32))
            order = range(n - 1, -1, -1) if d == 0 else range(n)
            for ri in range(2):
                col = (2 * d + ri) * 2 * SSM_STATE + e * SSM_STATE
                sp_ref[e, :, col:col + SSM_STATE] = jnp.concatenate([w[t][ri] for t in order], axis=0).astype(BF16)
            exps = range(1, n + 1) if d == 0 else range(n, 0, -1)
            q = [_cmul(cr_ref[d, e], ci_ref[d, e], pr[m:m + 1], pi[m:m + 1]) for m in exps]
            rows = slice(e * SSM_CK, (e + 1) * SSM_CK)
            cols = slice(e * SSM_STATE, (e + 1) * SSM_STATE)
            qt_ref[2 * d, rows, cols] = jnp.concatenate([x[0] for x in q], axis=0).astype(BF16)
            qt_ref[2 * d + 1, rows, cols] = (-jnp.concatenate([x[1] for x in q], axis=0)).astype(BF16)
        fwd, bwd = toeplitz
        acc = jnp.zeros((SSM_CK, SSM_CK), F32)
        for dlt in range(-(n - 1), n):
            if dlt > 0:
                src = fwd[dlt * c:(dlt + 1) * c]
            elif dlt < 0:
                src = bwd[-dlt * c:(-dlt + 1) * c]
            else:
                src = fwd[:c] + bwd[:c]
            acc = jnp.where(delta == dlt, jnp.concatenate([src] * n, axis=0), acc)
        intra_ref[e] = acc.astype(BF16)


def _ssm_tables(a_re, a_im, log_dt, b_re, b_im, c_re, c_im):
    depth = a_re.shape[0]
    dt = jnp.exp(log_dt)[..., None]
    zr, zi = a_re * dt, a_im * dt

    def powers(taus):
        taus = taus.astype(F32)[:, None]
        mag = jnp.exp(zr[..., None, :] * taus)
        return mag * jnp.cos(zi[..., None, :] * taus), mag * jnp.sin(zi[..., None, :] * taus)

    pr, pi = powers(jnp.arange(SSM_CHUNK + 1))
    lbr, lbi = pr[..., 1, :], pi[..., 1, :]
    den = a_re * a_re + a_im * a_im
    fr = ((lbr - 1.0) * a_re + lbi * a_im) / den
    fi = (lbi * a_re - (lbr - 1.0) * a_im) / den
    bbr, bbi = _cmul(fr[..., None, :], fi[..., None, :],
                     jnp.swapaxes(b_re, -1, -2), jnp.swapaxes(b_im, -1, -2))

    pair = lambda rows: pl.BlockSpec((None, 2, 2, rows, SSM_STATE), lambda l, p: (l, 0, p, 0, 0))
    intra, sproj, qproj_t = pl.pallas_call(
        _ssm_table_kernel,
        grid=(depth, SSM_PAIRS),
        in_specs=[pair(SSM_CHUNK + 1), pair(SSM_CHUNK + 1), pair(SSM_GROUP_DIM), pair(SSM_GROUP_DIM),
                  pair(SSM_GROUP_DIM), pair(SSM_GROUP_DIM)],
        out_specs=[pl.BlockSpec((None, 2, SSM_CK, SSM_CK), lambda l, p: (l, p, 0, 0)),
                   pl.BlockSpec((None, None, 2, SSM_CK, 8 * SSM_STATE), lambda l, p: (l, p, 0, 0, 0)),
                   pl.BlockSpec((None, None, 4, 2 * SSM_CK, 2 * SSM_STATE), lambda l, p: (l, p, 0, 0, 0))],
        out_shape=[jax.ShapeDtypeStruct((depth, SSM_GROUPS, SSM_CK, SSM_CK), BF16),
                   jax.ShapeDtypeStruct((depth, SSM_PAIRS, 2, SSM_CK, 8 * SSM_STATE), BF16),
                   jax.ShapeDtypeStruct((depth, SSM_PAIRS, 4, 2 * SSM_CK, 2 * SSM_STATE), BF16)],
        compiler_params=_cparams(2),
        name="ssm_tables",
    )(pr, pi, bbr, bbi, c_re, c_im)

    sr, si = powers(jnp.arange(1, 9) * SSM_CHUNK)
    sr, si = sr.transpose(0, 1, 3, 2, 4), si.transpose(0, 1, 3, 2, 4)
    lam_pows = jnp.stack([jnp.stack([sr[:, 0], si[:, 0]], axis=1),
                          jnp.stack([sr[:, 1, ::-1], si[:, 1, ::-1]], axis=1)], axis=1)
    lam_pows = lam_pows.reshape(depth, 4 * 8, SSM_GROUPS * SSM_STATE)
    return intra, sproj, qproj_t, lam_pows


SSM_ROW_TILE = 1024


def _ssm_chunk_states(u_ref, sp_ref, st_ref):
    for r in range(u_ref.shape[1] // SSM_ROW_TILE):
        rows = slice(r * SSM_ROW_TILE, (r + 1) * SSM_ROW_TILE)
        st = (jnp.dot(u_ref[0, rows, :], sp_ref[0], preferred_element_type=F32)
              + jnp.dot(u_ref[1, rows, :], sp_ref[1], preferred_element_type=F32))
        for part in range(4):
            st_ref[part, rows, :] = st[:, 128 * part:128 * (part + 1)]


def _ssm_carry_states(st_ref, lam_ref, xin_ref, n_seq):
    n_chunks, lanes = st_ref.shape[1] // n_seq, st_ref.shape[2]
    n_tiles = n_chunks // 8
    row = lax.broadcasted_iota(jnp.int32, (8, lanes), 0)

    def bcast(x, r):
        return jnp.broadcast_to(x[r:r + 1, :], (8, lanes))

    tabs = []
    for d in range(2):
        pr, pi = lam_ref[16 * d:16 * d + 8, :], lam_ref[16 * d + 8:16 * d + 16, :]
        at = (lambda m: m - 1) if d == 0 else (lambda m: 8 - m)
        tabs.append((pr, pi, [(bcast(pr, at(m)), bcast(pi, at(m))) for m in (1, 2, 4)]))

    def shifted(x, dist, forward):
        if forward:
            return jnp.where(row >= dist, pltpu.roll(x, dist, 0), 0.0)
        return jnp.where(row < 8 - dist, pltpu.roll(x, 8 - dist, 0), 0.0)

    def scan_tile(sr, si, cr, ci, d):
        forward = d == 0
        pr, pi, steps = tabs[d]
        hr, hi = sr, si
        for (lr, li), dist in zip(steps, (1, 2, 4)):
            mr, mi = _cmul(lr, li, shifted(hr, dist, forward), shifted(hi, dist, forward))
            hr, hi = hr + mr, hi + mi
        mr, mi = _cmul(pr, pi, cr, ci)
        er, ei = hr + mr, hi + mi
        if forward:
            xr = jnp.where(row >= 1, pltpu.roll(er, 1, 0), cr)
            xi = jnp.where(row >= 1, pltpu.roll(ei, 1, 0), ci)
            return xr, xi, bcast(er, 7), bcast(ei, 7)
        xr = jnp.where(row < 7, pltpu.roll(er, 7, 0), cr)
        xi = jnp.where(row < 7, pltpu.roll(ei, 7, 0), ci)
        return xr, xi, bcast(er, 0), bcast(ei, 0)

    def body(i, carry):
        new = []
        for s in range(n_seq):
            for d in range(2):
                cr, ci = carry[2 * (2 * s + d)], carry[2 * (2 * s + d) + 1]
                r0 = pl.multiple_of(s * n_chunks + (i if d == 0 else n_tiles - 1 - i) * 8, 8)
                sr, si = st_ref[2 * d, pl.ds(r0, 8), :], st_ref[2 * d + 1, pl.ds(r0, 8), :]
                xr, xi, ncr, nci = scan_tile(sr, si, cr, ci, d)
                xin_ref[2 * d, pl.ds(r0, 8), :] = xr
                xin_ref[2 * d + 1, pl.ds(r0, 8), :] = xi
                new += [ncr, nci]
        return tuple(new)

    zero = jnp.zeros((8, lanes), F32)
    lax.fori_loop(0, n_tiles, body, (zero,) * (4 * n_seq))


def _ssm_outputs(u_ref, xin_ref, mi_ref, qt_ref, o_ref):
    lanes_contract = (((1,), (1,)), ((), ()))
    for r in range(u_ref.shape[1] // SSM_ROW_TILE):
        rows = slice(r * SSM_ROW_TILE, (r + 1) * SSM_ROW_TILE)
        acc = lax.dot_general(xin_ref[0, rows, :].astype(BF16), qt_ref[0], lanes_contract,
                              preferred_element_type=F32)
        for part in range(1, 4):
            acc = acc + lax.dot_general(xin_ref[part, rows, :].astype(BF16), qt_ref[part], lanes_contract,
                                        preferred_element_type=F32)
        for e in range(2):
            y0 = jnp.dot(u_ref[e, rows, :], mi_ref[e], preferred_element_type=F32)
            o_ref[e, rows, :] = y0 + acc[:, SSM_CK * e:SSM_CK * (e + 1)]


def _ssm_pair_kernel(u_ref, sp_ref, lam_ref, mi_ref, qt_ref, o_ref, st_ref, xin_ref, *, n_seq):
    _ssm_chunk_states(u_ref, sp_ref, st_ref)
    _ssm_carry_states(st_ref, lam_ref, xin_ref, n_seq)
    _ssm_outputs(u_ref, xin_ref, mi_ref, qt_ref, o_ref)


def _ssm_scan(upack, batch, seq, tables, layer):
    intra, sproj, qproj_t, lam_pows = tables
    rows = (seq // SSM_CHUNK) * batch
    lanes = 2 * SSM_STATE
    pair_rows = pl.BlockSpec((2, rows, SSM_CK), lambda p: (p, 0, 0))
    return pl.pallas_call(
        functools.partial(_ssm_pair_kernel, n_seq=batch),
        grid=(SSM_PAIRS,),
        in_specs=[pair_rows,
                  pl.BlockSpec((None, None, 2, SSM_CK, 4 * lanes), lambda p: (layer, p, 0, 0, 0)),
                  pl.BlockSpec((None, 32, lanes), lambda p: (layer, 0, p)),
                  pl.BlockSpec((None, 2, SSM_CK, SSM_CK), lambda p: (layer, p, 0, 0)),
                  pl.BlockSpec((None, None, 4, 2 * SSM_CK, lanes), lambda p: (layer, p, 0, 0, 0))],
        out_specs=pair_rows,
        out_shape=jax.ShapeDtypeStruct((SSM_GROUPS, rows, SSM_CK), F32),
        scratch_shapes=[pltpu.VMEM((4, rows, lanes), F32), pltpu.VMEM((4, rows, lanes), F32)],
        compiler_params=_cparams(1),
        name="ssm_pair",
    )(upack, sproj, lam_pows, intra, qproj_t)


def _merge_tile(x_ref, gate_ref, fm_ref, na_ref, ypack_ref, ussm_ref, dskip_ref, wglu_ref,
                wfn_ref, wna_ref, wssm_ref, wout_ref, ytok_ref, yrow_ref):
    y_fn = jnp.dot(fm_ref[...], wfn_ref[...], preferred_element_type=F32)
    y_na = jnp.dot(na_ref[...], wna_ref[...], preferred_element_type=F32)
    merged = (gate_ref[:, :D_MODEL].astype(F32) * y_fn
              + gate_ref[:, D_MODEL:2 * D_MODEL].astype(F32) * y_na)
    for t in range(SSM_CHUNK):
        for g in range(SSM_GROUPS):
            yrow_ref[:, g * SSM_GROUP_DIM:(g + 1) * SSM_GROUP_DIM] = (
                ypack_ref[g, :, t * SSM_GROUP_DIM:(t + 1) * SSM_GROUP_DIM])
        ytok_ref[:, t, :] = yrow_ref[...]
    yscan = ytok_ref[...].reshape(TOKEN_TILE, SSM_WIDTH)
    y = yscan + dskip_ref[...] * ussm_ref[...]
    y = jax.nn.gelu(y)
    y = y * jax.nn.sigmoid(jnp.dot(y.astype(BF16), wglu_ref[...], preferred_element_type=F32))
    y_ssm = jnp.dot(y.astype(BF16), wssm_ref[...], preferred_element_type=F32)
    merged = merged + gate_ref[:, 2 * D_MODEL:].astype(F32) * y_ssm
    return x_ref[...] + jnp.dot(merged.astype(BF16), wout_ref[...], preferred_element_type=F32)


def _ffn_tile(x, g_ref, wup_ref, wdown_ref, gfin_ref, up_ref, final_norm):
    h = _rms(x, g_ref[...]).astype(BF16)
    for c in range(D_FF // D_MODEL):
        cols = slice(c * D_MODEL, (c + 1) * D_MODEL)
        a = jnp.maximum(jnp.dot(h, wup_ref[:, cols], preferred_element_type=F32), 0.0)
        up_ref[:, cols] = (a * a).astype(BF16)
    y = x + jnp.dot(up_ref[...], wdown_ref[...], preferred_element_type=F32)
    if final_norm:
        y = _rms(y, gfin_ref[...])
    return y


def _mix_ffn_kernel(x_ref, gate_ref, fm_ref, na_ref, ypack_ref, ussm_ref, dskip_ref, wglu_ref,
                    wfn_ref, wna_ref, wssm_ref, wout_ref, g_ref, wup_ref, wdown_ref, gfin_ref,
                    o_ref, ytok_ref, yrow_ref, up_ref, *, final_norm):
    x1 = _merge_tile(x_ref, gate_ref, fm_ref, na_ref, ypack_ref, ussm_ref, dskip_ref, wglu_ref,
                     wfn_ref, wna_ref, wssm_ref, wout_ref, ytok_ref, yrow_ref)
    o_ref[...] = _ffn_tile(x1, g_ref, wup_ref, wdown_ref, gfin_ref, up_ref, final_norm)


def _mix_ffn(x, gates, fm, na, ypack, ussm, ssm_d, w_glu, w_br_fn, w_br_na, w_br_ssm, w_out,
             g_ffn, w_up, w_down, g_final, layer, final_norm):
    n = x.shape[0]
    tm = TOKEN_TILE
    row = lambda width: pl.BlockSpec((tm, width), lambda i: (i, 0))
    wspec = lambda k, m: _resident((None, k, m), lambda i: (layer, 0, 0))
    return pl.pallas_call(
        functools.partial(_mix_ffn_kernel, final_norm=final_norm),
        grid=(n // tm,),
        in_specs=[row(D_MODEL), row(N_BRANCH * D_MODEL), row(FN_WIDTH), row(NA_WIDTH),
                  pl.BlockSpec((SSM_GROUPS, tm // SSM_CHUNK, SSM_CK), lambda i: (0, i, 0)),
                  row(SSM_WIDTH), wspec(1, SSM_WIDTH), wspec(SSM_WIDTH, SSM_WIDTH),
                  wspec(FN_WIDTH, D_MODEL), wspec(NA_WIDTH, D_MODEL), wspec(SSM_WIDTH, D_MODEL),
                  wspec(D_MODEL, D_MODEL),
                  wspec(1, D_MODEL), wspec(D_MODEL, D_FF), wspec(D_FF, D_MODEL),
                  _resident((1, D_MODEL), lambda i: (0, 0))],
        out_specs=row(D_MODEL),
        out_shape=jax.ShapeDtypeStruct((n, D_MODEL), F32),
        scratch_shapes=[pltpu.VMEM((tm // SSM_CHUNK, SSM_CHUNK, SSM_WIDTH), F32),
                        pltpu.VMEM((tm // SSM_CHUNK, SSM_WIDTH), F32),
                        pltpu.VMEM((tm, D_FF), BF16)],
        compiler_params=_cparams(1),
        name="mix_ffn",
    )(x, gates, fm, na, ypack, ussm, ssm_d, w_glu, w_br_fn, w_br_na, w_br_ssm, w_out,
      g_ffn, w_up, w_down, g_final)


def kernel(x, g_mix, w_in, na_rpb, ssm_a_re, ssm_a_im, ssm_log_dt, ssm_b_re, ssm_b_im, ssm_c_re, ssm_c_im,
           ssm_d, w_glu, w_br_fn, w_br_na, w_br_ssm, w_out, g_ffn, w_up, w_down, g_final):
    batch, seq, _ = x.shape
    depth = w_in.shape[0]
    assert seq == DFT_N1 * DFT_N2 and seq % (GRID_W * NA_RQ) == 0 and (batch * seq) % TOKEN_TILE == 0
    rows = seq // GRID_W

    to_bf16 = lambda w: w.astype(BF16)
    w_in, w_glu, w_br_fn, w_br_na, w_br_ssm, w_out, w_up, w_down = map(
        to_bf16, (w_in, w_glu, w_br_fn, w_br_na, w_br_ssm, w_out, w_up, w_down))
    g_mix3 = g_mix.reshape(depth, 1, D_MODEL)
    g_ffn3 = g_ffn.reshape(depth, 1, D_MODEL)
    ssm_d3 = ssm_d.reshape(depth, 1, SSM_WIDTH)
    g_final2 = g_final.reshape(1, D_MODEL)
    dft_tables = _dft_tables(seq)
    na_slabs = _na_bias_slabs(na_rpb)
    na_row_masks = jnp.asarray(_na_row_masks(rows)).astype(BF16)
    ssm_tables = _ssm_tables(ssm_a_re, ssm_a_im, ssm_log_dt, ssm_b_re, ssm_b_im, ssm_c_re, ssm_c_im)

    xs = x.reshape(batch * seq, D_MODEL)
    for l in range(depth):
        ufn, q, k, v, ussm, upack, gates = _inproj(xs, g_mix3, w_in, l)
        fm = _fourier_mix(ufn, batch, seq, dft_tables)
        na = _neighbourhood_attention(q, k, v, na_slabs, na_row_masks, l, batch, seq)
        ypack = _ssm_scan(upack, batch, seq, ssm_tables, l)
        xs = _mix_ffn(xs, gates, fm, na, ypack, ussm, ssm_d3, w_glu, w_br_fn, w_br_na, w_br_ssm, w_out,
                      g_ffn3, w_up, w_down, g_final2, l, final_norm=(l == depth - 1))
    return xs.reshape(batch, seq, D_MODEL)
```
